```python
import jax, jax.numpy as jnp
from jax import lax
import numpy as np

D_MODEL = 1024
BATCH = 16
SEQ = 256
DEPTH = 4
DEC_BATCH = 4
DEC_SEQ = 4096
PAST_LEN = 512

GRID_W = 64
N_MIXERS = 3
N_HEADS = 16
N_KV_HEADS = 4
GROUP = N_HEADS // N_KV_HEADS
HEAD_DIM = 64
AXIS_DIM = HEAD_DIM // 2
AXIS_PAIRS = AXIS_DIM // 2
ROPE_THETA = 10000.0
QKV_DIM = (N_HEADS + 2 * N_KV_HEADS) * HEAD_DIM
QBLK = 128
WINDOW = 128
BAND = QBLK + 2 * WINDOW
KH_MAX = 8
KW = 16
N_EXPERTS = 16
CAPACITY_FACTOR = 2
D_EXPERT = D_MODEL
N_SINK_LAYERS = (DEPTH + 1) // 3
N_NA_LAYERS = DEPTH // 3
SCALE = HEAD_DIM ** -0.5
NEG_INF = -1e30
EPS = 1e-6

kernel_name = "hybrid_diffusion_interleaved_attn_ecmoe_step"


def rmsnorm(x, g):
    xf = x.astype(jnp.float32)
    y = xf * lax.rsqrt(jnp.mean(xf * xf, axis=-1, keepdims=True) + EPS)
    return (y * g.astype(jnp.float32)).astype(x.dtype)


def ada_modulation(cond, w, b):
    m = jax.nn.silu(cond) @ w + b
    return jnp.split(m[..., None, :], 6, axis=-1)


def modulated_rmsnorm(x, g, shift, scale):
    return rmsnorm(x, g) * (1 + scale) + shift


def qkv_heads(h, w_qkv, q_g, k_g):
    b, n, _ = h.shape
    qkv = h @ w_qkv
    q = qkv[..., :N_HEADS * HEAD_DIM].reshape(b, n, N_HEADS, HEAD_DIM)
    k = qkv[..., N_HEADS * HEAD_DIM:(N_HEADS + N_KV_HEADS) * HEAD_DIM].reshape(b, n, N_KV_HEADS, HEAD_DIM)
    v = qkv[..., (N_HEADS + N_KV_HEADS) * HEAD_DIM:].reshape(b, n, N_KV_HEADS, HEAD_DIM)
    return rmsnorm(q, q_g), rmsnorm(k, k_g), v


def axial_rope(x):
    n = x.shape[1]
    t = jnp.arange(n)
    inv_freq = ROPE_THETA ** (-jnp.arange(AXIS_PAIRS, dtype=jnp.float32) / AXIS_PAIRS)

    def rot(xh, pos):
        ang = pos.astype(jnp.float32)[:, None] * inv_freq[None, :]
        cos = jnp.cos(ang)[None, :, None, :].astype(x.dtype)
        sin = jnp.sin(ang)[None, :, None, :].astype(x.dtype)
        x1, x2 = xh[..., :AXIS_PAIRS], xh[..., AXIS_PAIRS:]
        return jnp.concatenate([x1 * cos - x2 * sin, x2 * cos + x1 * sin], axis=-1)

    return jnp.concatenate([rot(x[..., :AXIS_DIM], t // GRID_W), rot(x[..., AXIS_DIM:], t % GRID_W)], axis=-1)


def attn_probs(s, sink):
    if sink is None:
        return jax.nn.softmax(s, axis=-1)
    sk = sink.astype(jnp.float32).reshape(N_KV_HEADS, GROUP)[:, :, None, None]
    m = jnp.maximum(jnp.max(s, axis=-1, keepdims=True), sk)
    e = jnp.exp(s - m)
    return e / (jnp.sum(e, axis=-1, keepdims=True) + jnp.exp(sk - m))


def dense_attention(q, k, v, sink):
    b, n, _, _ = q.shape
    nb = n // QBLK
    qb = jnp.moveaxis(q.reshape(b, nb, QBLK, N_KV_HEADS, GROUP, HEAD_DIM), 1, 0)

    def one_block(qi):
        s = jnp.einsum("bqkgd,bskd->bkgqs", qi, k).astype(jnp.float32) * SCALE
        p = attn_probs(s, sink).astype(v.dtype)
        return jnp.einsum("bkgqs,bskd->bqkgd", p, v)

    o = lax.map(one_block, qb)
    return jnp.moveaxis(o, 0, 1).reshape(b, n, N_HEADS * HEAD_DIM)


def windowed_attention(q, k, v, k_ctx, v_ctx, sink):
    b, n, _, _ = q.shape
    nb = n // QBLK
    pad = ((0, 0), (WINDOW, WINDOW), (0, 0), (0, 0))
    kp, vp = jnp.pad(k, pad), jnp.pad(v, pad)
    qb = jnp.moveaxis(q.reshape(b, nb, QBLK, N_KV_HEADS, GROUP, HEAD_DIM), 1, 0)
    qi = jnp.arange(QBLK)[:, None]
    u = jnp.arange(BAND)[None, :]
    in_window = jnp.abs(u - WINDOW - qi) <= WINDOW

    def one_block(args):
        qj, j = args
        start = j * QBLK
        kb = lax.dynamic_slice_in_dim(kp, start, BAND, axis=1)
        vb = lax.dynamic_slice_in_dim(vp, start, BAND, axis=1)
        kpos = start - WINDOW + u
        valid = in_window & (kpos >= 0) & (kpos < n)
        s_loc = jnp.einsum("bqkgd,bskd->bkgqs", qj, kb).astype(jnp.float32) * SCALE
        s_loc = jnp.where(valid, s_loc, NEG_INF)
        s_ctx = jnp.einsum("bqkgd,bskd->bkgqs", qj, k_ctx).astype(jnp.float32) * SCALE
        p = attn_probs(jnp.concatenate([s_loc, s_ctx], axis=-1), sink).astype(v.dtype)
        return (jnp.einsum("bkgqs,bskd->bqkgd", p[..., :BAND], vb)
                + jnp.einsum("bkgqs,bskd->bqkgd", p[..., BAND:], v_ctx))

    o = lax.map(one_block, (qb, jnp.arange(nb)))
    return jnp.moveaxis(o, 0, 1).reshape(b, n, N_HEADS * HEAD_DIM)


def neighborhood_attention(q, k, v, k_ctx, v_ctx, rpb):
    b, n, _, _ = q.shape
    rows = n // GRID_W
    kh = min(KH_MAX, rows)
    nloc = kh * KW
    qg = jnp.moveaxis(q.reshape(b, rows, GRID_W, N_KV_HEADS, GROUP, HEAD_DIM), 1, 0)
    kgrid = k.reshape(b, rows, GRID_W, N_KV_HEADS, HEAD_DIM)
    vgrid = v.reshape(b, rows, GRID_W, N_KV_HEADS, HEAD_DIM)
    cols = jnp.arange(GRID_W)
    cstart = jnp.clip(cols - KW // 2, 0, GRID_W - KW)
    cidx = cstart[:, None] + jnp.arange(KW)[None, :]
    dc = cidx - cols[:, None] + (KW - 1)

    def gather(grid, rs):
        g = lax.dynamic_slice_in_dim(grid, rs, kh, axis=1)[:, :, cidx]
        return g.transpose(0, 2, 1, 3, 4, 5).reshape(b, GRID_W, nloc, N_KV_HEADS, HEAD_DIM)

    def one_row(args):
        qr, r = args
        rs = jnp.clip(r - kh // 2, 0, rows - kh)
        kn, vn = gather(kgrid, rs), gather(vgrid, rs)
        dr = rs + jnp.arange(kh) - r + (KH_MAX - 1)
        bias = rpb[:, dr[:, None, None], dc[None, :, :]]
        bias = bias.transpose(0, 2, 1, 3).reshape(N_KV_HEADS, GROUP, GRID_W, nloc).astype(jnp.float32)
        s_loc = jnp.einsum("bqkgd,bqskd->bkgqs", qr, kn).astype(jnp.float32) * SCALE + bias
        s_ctx = jnp.einsum("bqkgd,bskd->bkgqs", qr, k_ctx).astype(jnp.float32) * SCALE
        p = attn_probs(jnp.concatenate([s_loc, s_ctx], axis=-1), None).astype(v.dtype)
        return (jnp.einsum("bkgqs,bqskd->bqkgd", p[..., :nloc], vn)
                + jnp.einsum("bkgqs,bskd->bqkgd", p[..., nloc:], v_ctx))

    o = lax.map(one_row, (qg, jnp.arange(rows)))
    return jnp.moveaxis(o, 0, 1).reshape(b, n, N_HEADS * HEAD_DIM)


def expert_choice_ffn(h, w_router, w_gate, w_up, w_down):
    b, n, d = h.shape
    cap = (CAPACITY_FACTOR * n) // N_EXPERTS
    aff = jax.nn.softmax((h @ w_router).astype(jnp.float32), axis=-1)
    top_aff, idx = lax.top_k(jnp.swapaxes(aff, 1, 2), cap)
    xg = jax.vmap(lambda hb, ib: hb[ib])(h, idx)
    a = jnp.einsum("becd,edf->becf", xg, w_gate)
    g = jnp.einsum("becd,edf->becf", xg, w_up)
    y = jnp.einsum("becf,efd->becd", jax.nn.silu(a) * g, w_down) * top_aff[..., None].astype(h.dtype)
    return jax.vmap(lambda ib, yb: jnp.zeros((n, d), yb.dtype).at[ib.reshape(-1)].add(yb.reshape(-1, d)))(idx, y)


def setup_inputs(seed: int = 0) -> dict:
    key = jax.random.key(seed)
    ks = jax.random.split(key, 20)
    f32 = jnp.float32
    nrm = lambda k, shape, s: jax.random.normal(k, shape, f32) * s
    return {
        "x_prompt": nrm(ks[0], (BATCH, SEQ, D_MODEL), 1.0),
        "x_sample": nrm(ks[1], (DEC_BATCH, DEC_SEQ, D_MODEL), 1.0),
        "cache_k": nrm(ks[2], (DEC_BATCH, DEPTH, PAST_LEN, N_KV_HEADS, HEAD_DIM), 1.0),
        "cache_v": nrm(ks[3], (DEC_BATCH, DEPTH, PAST_LEN, N_KV_HEADS, HEAD_DIM), 1.0),
        "c": nrm(ks[4], (DEC_BATCH, D_MODEL), 1.0),
        "c_ctx": nrm(ks[5], (D_MODEL,), 1.0),
        "w_ada": nrm(ks[6], (DEPTH, D_MODEL, 6 * D_MODEL), 0.5 * D_MODEL ** -0.5),
        "b_ada": nrm(ks[7], (DEPTH, 6 * D_MODEL), 0.01),
        "norm_attn_g": 1.0 + nrm(ks[8], (DEPTH, D_MODEL), 0.01),
        "norm_ffn_g": 1.0 + nrm(ks[9], (DEPTH, D_MODEL), 0.01),
        "w_qkv": nrm(ks[10], (DEPTH, D_MODEL, QKV_DIM), D_MODEL ** -0.5),
        "q_norm_g": 1.0 + nrm(ks[11], (DEPTH, HEAD_DIM), 0.01),
        "k_norm_g": 1.0 + nrm(ks[12], (DEPTH, HEAD_DIM), 0.01),
        "w_o": nrm(ks[13], (DEPTH, N_HEADS * HEAD_DIM, D_MODEL), (N_HEADS * HEAD_DIM) ** -0.5),
        "sink_logits": nrm(ks[14], (N_SINK_LAYERS, N_HEADS), 0.5),
        "rel_pos_bias": nrm(ks[15], (N_NA_LAYERS, N_HEADS, 2 * KH_MAX - 1, 2 * KW - 1), 0.1),
        "w_router": nrm(ks[16], (DEPTH, D_MODEL, N_EXPERTS), D_MODEL ** -0.5),
        "w_gate": nrm(ks[17], (DEPTH, N_EXPERTS, D_MODEL, D_EXPERT), D_MODEL ** -0.5),
        "w_up": nrm(ks[18], (DEPTH, N_EXPERTS, D_MODEL, D_EXPERT), D_MODEL ** -0.5),
        "w_down": nrm(ks[19], (DEPTH, N_EXPERTS, D_EXPERT, D_MODEL), D_EXPERT ** -0.5),
    }


def reference(x_prompt, x_sample, cache_k, cache_v, c, c_ctx, w_ada, b_ada, norm_attn_g, norm_ffn_g,
              w_qkv, q_norm_g, k_norm_g, w_o, sink_logits, rel_pos_bias, w_router, w_gate, w_up, w_down):
    xp, xs = x_prompt, x_sample
    new_k, new_v = [], []
    for i in range(DEPTH):
        kind = i % N_MIXERS
        sink = sink_logits[i // N_MIXERS] if kind == 1 else None

        sa, ca, ga, sf, cf, gf = ada_modulation(c_ctx, w_ada[i], b_ada[i])
        h = modulated_rmsnorm(xp, norm_attn_g[i], sa, ca)
        q, k, v = qkv_heads(h, w_qkv[i], q_norm_g[i], k_norm_g[i])
        new_k.append(k)
        new_v.append(v)
        xp = xp + ga * (dense_attention(q, k, v, sink) @ w_o[i])
        h = modulated_rmsnorm(xp, norm_ffn_g[i], sf, cf)
        xp = xp + gf * expert_choice_ffn(h, w_router[i], w_gate[i], w_up[i], w_down[i])

        sa, ca, ga, sf, cf, gf = ada_modulation(c, w_ada[i], b_ada[i])
        k_ctx, v_ctx = cache_k[:, i], cache_v[:, i]
        h = modulated_rmsnorm(xs, norm_attn_g[i], sa, ca)
        q, k, v = qkv_heads(h, w_qkv[i], q_norm_g[i], k_norm_g[i])
        if kind == 0:
            o = dense_attention(axial_rope(q), jnp.concatenate([axial_rope(k), k_ctx], axis=1),
                                jnp.concatenate([v, v_ctx], axis=1), None)
        elif kind == 1:
            o = windowed_attention(axial_rope(q), axial_rope(k), v, k_ctx, v_ctx, sink)
        else:
            o = neighborhood_attention(q, k, v, k_ctx, v_ctx, rel_pos_bias[i // N_MIXERS])
        xs = xs + ga * (o @ w_o[i])
        h = modulated_rmsnorm(xs, norm_ffn_g[i], sf, cf)
        xs = xs + gf * expert_choice_ffn(h, w_router[i], w_gate[i], w_up[i], w_down[i])

    new_cache_k = jnp.stack(new_k, axis=1)
    new_cache_v = jnp.stack(new_v, axis=1)
    return (xp, xs, new_cache_k, new_cache_v)
```

```python
import functools

import jax
import jax.numpy as jnp
from jax import lax
from jax.experimental import pallas as pl
from jax.experimental.pallas import tpu as pltpu

F32 = jnp.float32
BF16 = jnp.bfloat16
I32 = jnp.int32

D_MODEL = 1024
DEPTH = 4
GRID_W = 64
N_HEADS = 16
N_KV_HEADS = 4
GROUP = N_HEADS // N_KV_HEADS
HEAD_DIM = 64
AXIS_PAIRS = HEAD_DIM // 4
ROPE_THETA = 10000.0
QKV_DIM = (N_HEADS + 2 * N_KV_HEADS) * HEAD_DIM
Q_DIM = N_HEADS * HEAD_DIM
KV_DIM = N_KV_HEADS * HEAD_DIM
WINDOW = 128
KH = 8
KW = 16
N_EXPERTS = 16
CAPACITY_FACTOR = 2
SCALE = HEAD_DIM ** -0.5
NEG_INF = -1e30
EPS = 1e-6

LANES = 128
TOK = 256
ROWS_PER_TOK = TOK // GRID_W
COND_ROWS = 8
D_HALF = D_MODEL // 2
FFN_TM = 512
UNROLL = 8
BISECT_STEPS = 64
VMEM_LIMIT = 56 * 1024 * 1024


def _cparams(sem):
    return pltpu.CompilerParams(dimension_semantics=sem, vmem_limit_bytes=VMEM_LIMIT)


def _split3(x):
    hi = x.astype(BF16)
    r = x - hi.astype(F32)
    mid = r.astype(BF16)
    lo = (r - mid.astype(F32)).astype(BF16)
    return hi, mid, lo


def _dot(a, b, dims=(((1,), (0,)), ((), ()))):
    return lax.dot_general(a, b, dims, preferred_element_type=F32)


_NT = (((1,), (1,)), ((), ()))
_TN = (((0,), (0,)), ((), ()))


def _dot_exact_rhs(onehot_bf16, x_f32, dims=(((1,), (0,)), ((), ()))):
    hi, mid, lo = _split3(x_f32)
    return _dot(onehot_bf16, hi, dims) + _dot(onehot_bf16, mid, dims) + _dot(onehot_bf16, lo, dims)


def _dot_f32(a, b, dims):
    a_hi, a_mid, _ = _split3(a)
    b_hi, b_mid, _ = _split3(b)
    return (_dot(a_hi, b_hi, dims) + _dot(a_hi, b_mid, dims) + _dot(a_mid, b_hi, dims))


def _mod_rmsnorm(x, g, shift, scale):
    ms = jnp.mean(x * x, axis=-1, keepdims=True)
    return (x * lax.rsqrt(ms + EPS) * g) * (1.0 + scale) + shift


ADA_TN = 1536


def _ada_kernel(cond_ref, w_ref, b_ref, out_ref):
    cnd = cond_ref[...]
    act = cnd * (1.0 / (1.0 + jnp.exp(-cnd)))
    out_ref[0] = _dot_f32(act, w_ref[0], (((1,), (0,)), ((), ()))) + b_ref[0]


def _ada_modulation(cond, w_ada, b_ada):
    n_out = 6 * D_MODEL
    return pl.pallas_call(
        _ada_kernel,
        grid=(DEPTH, n_out // ADA_TN),
        in_specs=[
            pl.BlockSpec((COND_ROWS, D_MODEL), lambda i, n: (0, 0)),
            pl.BlockSpec((1, D_MODEL, ADA_TN), lambda i, n: (i, 0, n)),
            pl.BlockSpec((1, 1, ADA_TN), lambda i, n: (i, 0, n)),
        ],
        out_specs=pl.BlockSpec((1, COND_ROWS, ADA_TN), lambda i, n: (i, 0, n)),
        out_shape=jax.ShapeDtypeStruct((DEPTH, COND_ROWS, n_out), F32),
        compiler_params=_cparams(("arbitrary", "arbitrary")),
        name="ada_modulation",
    )(cond, w_ada, b_ada.reshape(DEPTH, 1, n_out))


def _pre_attn_kernel(*refs, fuse_residual, rope, emit_cache):
    it = iter(refs)
    x_ref = next(it)
    if fuse_residual:
        acc_ref = next(it)
        pmod_ref = next(it)
    mod_ref = next(it)
    g_ref = next(it)
    w_ref = next(it)
    gq_ref = next(it)
    gk_ref = next(it)
    if rope:
        cos_ref = next(it)
        sin_ref = next(it)
    if fuse_residual:
        xo_ref = next(it)
    qT_ref = next(it)
    k_ref = next(it)
    vT_ref = next(it)
    if emit_cache:
        ck_ref = next(it)
        cv_ref = next(it)
    w_s = next(it)

    @pl.when((pl.program_id(0) == 0) & (pl.program_id(1) == 0))
    def _():
        w_s[...] = w_ref[...].astype(BF16)

    x = x_ref[0]
    if fuse_residual:
        x = x + pmod_ref[0, 5:6, :] * acc_ref[0]
        xo_ref[0] = x
    h = _mod_rmsnorm(x, g_ref[...], mod_ref[0, 0:1, :], mod_ref[0, 1:2, :])
    qkv = _dot(h.astype(BF16), w_s[...])
    if emit_cache:
        cv_ref[0] = qkv[:, Q_DIM + KV_DIM:]
    qkvT = qkv.T

    def head_norm(blk, gain):
        ss = jnp.sum(blk * blk, axis=0, keepdims=True)
        return blk * lax.rsqrt(ss * (1.0 / HEAD_DIM) + EPS) * gain

    def rotate(blk):
        if not rope:
            return blk
        cs = cos_ref[...]
        sn = sin_ref[...]
        parts = []
        for a in range(2):
            x1 = blk[32 * a:32 * a + 16]
            x2 = blk[32 * a + 16:32 * a + 32]
            c = cs[16 * a:16 * a + 16]
            s = sn[16 * a:16 * a + 16]
            parts.append(x1 * c - x2 * s)
            parts.append(x2 * c + x1 * s)
        return jnp.concatenate(parts, axis=0)

    gq = gq_ref[...]
    gk = gk_ref[...]
    for hd in range(N_HEADS):
        blk = qkvT[HEAD_DIM * hd:HEAD_DIM * (hd + 1)]
        qT_ref[0, hd] = (rotate(head_norm(blk, gq)) * SCALE).astype(BF16)
    kn_all = []
    for kv in range(N_KV_HEADS):
        off = Q_DIM + HEAD_DIM * kv
        kn_all.append(rotate(head_norm(qkvT[off:off + HEAD_DIM], gk)))
        offv = Q_DIM + KV_DIM + HEAD_DIM * kv
        vT_ref[0, kv, 0] = qkvT[offv:offv + HEAD_DIM].astype(BF16)
    kn_tok = jnp.concatenate(kn_all, axis=0).T
    for kv in range(N_KV_HEADS):
        k_ref[0, kv, 0] = kn_tok[:, HEAD_DIM * kv:HEAD_DIM * (kv + 1)].astype(BF16)
    if emit_cache:
        ck_ref[0] = kn_tok


def _pre_attn(x, acc, prev_mod, mod, mod_row, g, w_qkv, gq, gk, rope_tabs, emit_cache):
    b, n, _ = x.shape
    nt = n // TOK
    fuse = acc is not None
    rope = rope_tabs is not None
    tok_spec = pl.BlockSpec((1, TOK, D_MODEL), lambda bi, t: (bi, t, 0))
    mod_spec = pl.BlockSpec((1, 6, D_MODEL), lambda bi, t: (mod_row(bi), 0, 0))
    ins, in_specs = [x], [tok_spec]
    if fuse:
        ins += [acc, prev_mod]
        in_specs += [tok_spec, mod_spec]
    ins += [mod, g.reshape(1, D_MODEL), w_qkv,
            jnp.broadcast_to(gq[:, None], (HEAD_DIM, TOK)),
            jnp.broadcast_to(gk[:, None], (HEAD_DIM, TOK))]
    in_specs += [
        mod_spec,
        pl.BlockSpec((1, D_MODEL), lambda bi, t: (0, 0)),
        pl.BlockSpec((D_MODEL, QKV_DIM), lambda bi, t: (0, 0)),
        pl.BlockSpec((HEAD_DIM, TOK), lambda bi, t: (0, 0)),
        pl.BlockSpec((HEAD_DIM, TOK), lambda bi, t: (0, 0)),
    ]
    if rope:
        ins += list(rope_tabs)
        in_specs += [pl.BlockSpec((2 * AXIS_PAIRS, TOK), lambda bi, t: (0, t))] * 2
    out_shape, out_specs = [], []
    if fuse:
        out_shape.append(jax.ShapeDtypeStruct((b, n, D_MODEL), F32))
        out_specs.append(tok_spec)
    out_shape += [
        jax.ShapeDtypeStruct((b, N_HEADS, HEAD_DIM, n), BF16),
        jax.ShapeDtypeStruct((b, N_KV_HEADS, nt, TOK, HEAD_DIM), BF16),
        jax.ShapeDtypeStruct((b, N_KV_HEADS, nt, HEAD_DIM, TOK), BF16),
    ]
    out_specs += [
        pl.BlockSpec((1, N_HEADS, HEAD_DIM, TOK), lambda bi, t: (bi, 0, 0, t)),
        pl.BlockSpec((1, N_KV_HEADS, 1, TOK, HEAD_DIM), lambda bi, t: (bi, 0, t, 0, 0)),
        pl.BlockSpec((1, N_KV_HEADS, 1, HEAD_DIM, TOK), lambda bi, t: (bi, 0, t, 0, 0)),
    ]
    if emit_cache:
        out_shape += [jax.ShapeDtypeStruct((b, n, KV_DIM), F32)] * 2
        out_specs += [pl.BlockSpec((1, TOK, KV_DIM), lambda bi, t: (bi, t, 0))] * 2
    outs = pl.pallas_call(
        functools.partial(_pre_attn_kernel, fuse_residual=fuse, rope=rope, emit_cache=emit_cache),
        grid=(b, nt),
        in_specs=in_specs,
        out_specs=out_specs,
        out_shape=out_shape,
        scratch_shapes=[pltpu.VMEM((D_MODEL, QKV_DIM), BF16)],
        compiler_params=_cparams(("arbitrary", "arbitrary")),
        name="pre_attn",
    )(*ins)
    outs = list(outs)
    x_new = outs.pop(0) if fuse else x
    return [x_new] + outs


def _attn_kernel(*refs, mode, n_local, has_ctx, has_sink, bias_heads):
    it = iter(refs)
    qT_ref = next(it)
    if mode == "dense":
        k_ref = next(it)
        vT_ref = next(it)
    else:
        k_refs = [next(it) for _ in range(3)]
        vT_refs = [next(it) for _ in range(3)]
        bias_ref = next(it)
    if has_ctx:
        kc_ref = next(it)
        vcT_ref = next(it)
    if has_sink:
        sink_ref = next(it)
    oT_ref = next(it)
    m_s = next(it)
    l_s = next(it)
    acc_s = next(it)

    kv = pl.program_id(1)
    for hd in range(GROUP):
        if has_sink:
            m_s[hd] = jnp.full((1, TOK), sink_ref[kv * GROUP + hd], F32)
            l_s[hd] = jnp.ones((1, TOK), F32)
        else:
            m_s[hd] = jnp.full((1, TOK), NEG_INF, F32)
            l_s[hd] = jnp.zeros((1, TOK), F32)
        acc_s[hd] = jnp.zeros((HEAD_DIM, TOK), F32)

    def chunk(kch, vch, bias_of_head):
        for hd in range(GROUP):
            s = _dot(kch, qT_ref[0, hd])
            if bias_of_head is not None:
                s = s + bias_of_head(hd)
            m_prev = m_s[hd]
            m_new = jnp.maximum(m_prev, jnp.max(s, axis=0, keepdims=True))
            alpha = jnp.exp(m_prev - m_new)
            p = jnp.exp(s - m_new)
            l_s[hd] = alpha * l_s[hd] + jnp.sum(p, axis=0, keepdims=True)
            acc_s[hd] = alpha * acc_s[hd] + _dot(vch, p.astype(BF16))
            m_s[hd] = m_new

    if mode == "dense":
        if n_local == 1:
            chunk(k_ref[0, 0, 0], vT_ref[0, 0, 0], None)
        else:
            def body(c, carry):
                chunk(k_ref[0, 0, c], vT_ref[0, 0, c], None)
                return carry
            lax.fori_loop(0, n_local, body, 0)
    else:
        for slot in range(3):
            if bias_heads == 1:
                bias_fn = lambda hd, slot=slot: bias_ref[0, 0, slot]
            else:
                bias_fn = lambda hd, slot=slot: bias_ref[0, hd, slot]
            chunk(k_refs[slot][0, 0, 0], vT_refs[slot][0, 0, 0], bias_fn)
    if has_ctx:
        for c in range(kc_ref.shape[2]):
            chunk(kc_ref[0, 0, c], vcT_ref[0, 0, c], None)

    for hd in range(GROUP):
        oT_ref[0, hd] = (acc_s[hd] / l_s[hd]).astype(BF16)


def _attention(qT, k, vT, ctx, sink, bias):
    b, _, _, n = qT.shape
    nt = n // TOK
    mode = "dense" if bias is None else "band"
    ins = [qT]
    in_specs = [pl.BlockSpec((1, GROUP, HEAD_DIM, TOK), lambda bi, kv, j: (bi, kv, 0, j))]
    if mode == "dense":
        ins += [k, vT]
        in_specs += [
            pl.BlockSpec((1, 1, nt, TOK, HEAD_DIM), lambda bi, kv, j: (bi, kv, 0, 0, 0)),
            pl.BlockSpec((1, 1, nt, HEAD_DIM, TOK), lambda bi, kv, j: (bi, kv, 0, 0, 0)),
        ]
        bias_heads = 0
    else:
        def base(j):
            return jnp.clip(j - 1, 0, nt - 3)
        ins += [k, k, k, vT, vT, vT, bias]
        for slot in range(3):
            in_specs.append(pl.BlockSpec(
                (1, 1, 1, TOK, HEAD_DIM),
                lambda bi, kv, j, slot=slot: (bi, kv, base(j) + slot, 0, 0)))
        for slot in range(3):
            in_specs.append(pl.BlockSpec(
                (1, 1, 1, HEAD_DIM, TOK),
                lambda bi, kv, j, slot=slot: (bi, kv, base(j) + slot, 0, 0)))
        bias_heads = 1 if bias.shape[1] == 1 else GROUP

        def variant(j):
            return (j > 0).astype(I32) + (j == nt - 1).astype(I32)
        if bias_heads == 1:
            in_specs.append(pl.BlockSpec((1, 1, 3, TOK, TOK), lambda bi, kv, j: (variant(j), 0, 0, 0, 0)))
        else:
            in_specs.append(pl.BlockSpec((1, GROUP, 3, TOK, TOK), lambda bi, kv, j: (variant(j), kv, 0, 0, 0)))
    if ctx is not None:
        kc, vcT = ctx
        nc = kc.shape[2]
        ins += [kc, vcT]
        in_specs += [
            pl.BlockSpec((1, 1, nc, TOK, HEAD_DIM), lambda bi, kv, j: (bi, kv, 0, 0, 0)),
            pl.BlockSpec((1, 1, nc, HEAD_DIM, TOK), lambda bi, kv, j: (bi, kv, 0, 0, 0)),
        ]
    if sink is not None:
        ins.append(sink)
        in_specs.append(pl.BlockSpec(memory_space=pltpu.SMEM))
    return pl.pallas_call(
        functools.partial(_attn_kernel, mode=mode, n_local=nt, has_ctx=ctx is not None,
                          has_sink=sink is not None, bias_heads=bias_heads),
        grid=(b, N_KV_HEADS, nt),
        in_specs=in_specs,
        out_specs=pl.BlockSpec((1, GROUP, HEAD_DIM, TOK), lambda bi, kv, j: (bi, kv, 0, j)),
        out_shape=jax.ShapeDtypeStruct((b, N_HEADS, HEAD_DIM, n), BF16),
        scratch_shapes=[
            pltpu.VMEM((GROUP, 1, TOK), F32),
            pltpu.VMEM((GROUP, 1, TOK), F32),
            pltpu.VMEM((GROUP, HEAD_DIM, TOK), F32),
        ],
        compiler_params=_cparams(("arbitrary", "arbitrary", "arbitrary")),
        name="attention_" + mode,
    )(*ins)


def _post_attn_kernel(oT_ref, x_ref, mod_ref, g_ref, wo_ref, wrT_ref, xm_ref, h2_ref, affT_ref, wo_s):
    @pl.when((pl.program_id(0) == 0) & (pl.program_id(1) == 0))
    def _():
        wo_s[...] = wo_ref[...].astype(BF16)

    oT = oT_ref[0].reshape(Q_DIM, TOK)
    y = _dot(oT, wo_s[...], _TN)
    xm = x_ref[0] + mod_ref[0, 2:3, :] * y
    xm_ref[0] = xm
    h2 = _mod_rmsnorm(xm, g_ref[...], mod_ref[0, 3:4, :], mod_ref[0, 4:5, :])
    h2_ref[0] = h2
    logitsT = _dot_f32(wrT_ref[...], h2, _NT)
    mx = jnp.max(logitsT, axis=0, keepdims=True)
    e = jnp.exp(logitsT - mx)
    affT_ref[0] = e / jnp.sum(e, axis=0, keepdims=True)


def _post_attn(oT, x, mod, mod_row, g, w_o, w_router):
    b, n, _ = x.shape
    nt = n // TOK
    tok_spec = pl.BlockSpec((1, TOK, D_MODEL), lambda bi, t: (bi, t, 0))
    return pl.pallas_call(
        _post_attn_kernel,
        grid=(b, nt),
        in_specs=[
            pl.BlockSpec((1, N_HEADS, HEAD_DIM, TOK), lambda bi, t: (bi, 0, 0, t)),
            tok_spec,
            pl.BlockSpec((1, 6, D_MODEL), lambda bi, t: (mod_row(bi), 0, 0)),
            pl.BlockSpec((1, D_MODEL), lambda bi, t: (0, 0)),
            pl.BlockSpec((Q_DIM, D_MODEL), lambda bi, t: (0, 0)),
            pl.BlockSpec((N_EXPERTS, D_MODEL), lambda bi, t: (0, 0)),
        ],
        out_specs=[tok_spec, tok_spec, pl.BlockSpec((1, N_EXPERTS, TOK), lambda bi, t: (bi, 0, t))],
        out_shape=[
            jax.ShapeDtypeStruct((b, n, D_MODEL), F32),
            jax.ShapeDtypeStruct((b, n, D_MODEL), F32),
            jax.ShapeDtypeStruct((b, N_EXPERTS, n), F32),
        ],
        scratch_shapes=[pltpu.VMEM((Q_DIM, D_MODEL), BF16)],
        compiler_params=_cparams(("arbitrary", "arbitrary")),
        name="post_attn",
    )(oT, x, mod, g.reshape(1, D_MODEL), w_o, w_router.T)


def _select_kernel(a_ref, idx_ref, gate_ref, *, groups, chunks, cap):
    rows = groups * chunks
    slots = max(cap, LANES)
    a = a_ref[0]

    ri = lax.broadcasted_iota(I32, (rows, rows), 0)
    ci = lax.broadcasted_iota(I32, (rows, rows), 1)
    same = (ri // chunks) == (ci // chunks)
    blockdiag = jnp.where(same, 1.0, 0.0).astype(BF16)
    lower = jnp.where(same & (ci < ri), 1.0, 0.0).astype(BF16)
    li = lax.broadcasted_iota(I32, (LANES, LANES), 0)
    lj = lax.broadcasted_iota(I32, (LANES, LANES), 1)
    upper_incl = jnp.where(li <= lj, 1.0, 0.0).astype(BF16)
    ones_ll = jnp.ones((LANES, LANES), BF16)

    def as_bf16(mask):
        return jnp.where(mask, 1.0, 0.0).astype(BF16)

    def row_total(x01):
        return _dot(x01, ones_ll)

    def expert_total(x01):
        return _dot(blockdiag, row_total(x01).astype(BF16))

    capf = float(cap)

    def search(step, bounds):
        lo, hi = bounds
        mid = 0.5 * (lo + hi)
        ok = expert_total(as_bf16(a >= mid)) >= capf
        return jnp.where(ok, mid, lo), jnp.where(ok, hi, mid)

    lo, hi = lax.fori_loop(0, BISECT_STEPS, search,
                           (jnp.zeros((rows, LANES), F32), jnp.full((rows, LANES), 2.0, F32)))

    gt = a >= hi
    eq = (a >= lo) & (a < hi)
    need = capf - expert_total(as_bf16(gt))
    eq01 = as_bf16(eq)
    eq_rank = (_dot(eq01, upper_incl) - eq01.astype(F32)
               + _dot(lower, row_total(eq01).astype(BF16)))
    sel = gt | (eq & (eq_rank < need))
    sel01 = as_bf16(sel)
    sel_rows = row_total(sel01)
    c_excl = _dot(lower, sel_rows.astype(BF16))
    c_incl = c_excl + sel_rows
    key = 2.0 * (_dot(sel01, upper_incl) + c_excl) - sel01.astype(F32)
    first_lane = as_bf16(lax.broadcasted_iota(I32, (8, LANES), 1) == 0)
    c_excl_l = _dot_exact_rhs(first_lane, c_excl, _NT)[0:1, :]
    c_incl_l = _dot_exact_rhs(first_lane, c_incl, _NT)[0:1, :]

    jcol = lax.broadcasted_iota(I32, (slots, rows), 0).astype(F32)
    rlane = lax.broadcasted_iota(I32, (slots, rows), 1)
    jl = lax.broadcasted_iota(I32, (slots, LANES), 0).astype(F32)
    lane8 = lax.broadcasted_iota(I32, (8, LANES), 1).astype(BF16)
    chunk8 = (lax.broadcasted_iota(I32, (8, rows), 1) % chunks).astype(BF16)
    ones8 = jnp.ones((8, LANES), BF16)

    def per_group(e, carry):
        mine = (rlane // chunks) == e
        onehot = as_bf16(mine & (c_excl_l <= jcol) & (jcol < c_incl_l))
        hit = _dot_exact_rhs(onehot, key) == 2.0 * jl + 1.0
        hit01 = as_bf16(hit)
        tok = _dot(chunk8, onehot, _NT) * float(LANES) + _dot(lane8, hit01, _NT)
        gsel = jnp.where(hit, _dot_exact_rhs(onehot, a), 0.0)
        g = _dot_exact_rhs(ones8, gsel, _NT)
        idx_ref[0, pl.ds(e, 1), :] = tok[0:1, :cap].astype(I32)
        gate_ref[0, pl.ds(e, 1), :] = g[0:1, :cap]
        return carry

    lax.fori_loop(0, groups, per_group, 0)


def _select(affT, cap):
    b, _, n = affT.shape
    chunks = n // LANES
    bstep = b if b * N_EXPERTS * chunks <= 512 else 1
    groups = bstep * N_EXPERTS
    rows = groups * chunks
    idx, gate = pl.pallas_call(
        functools.partial(_select_kernel, groups=groups, chunks=chunks, cap=cap),
        grid=(b // bstep,),
        in_specs=[pl.BlockSpec((1, rows, LANES), lambda bi: (bi, 0, 0))],
        out_specs=[pl.BlockSpec((1, groups, cap), lambda bi: (bi, 0, 0))] * 2,
        out_shape=[jax.ShapeDtypeStruct((b // bstep, groups, cap), I32),
                   jax.ShapeDtypeStruct((b // bstep, groups, cap), F32)],
        compiler_params=_cparams(("arbitrary",)),
        name="select",
    )(affT.reshape(b // bstep, rows, LANES))
    return idx.reshape(b, N_EXPERTS, cap), gate.reshape(b, N_EXPERTS, cap)


def _dispatch_kernel(idx_ref, h_ref, xg_ref, buf, *, cap, eg):
    bi = pl.program_id(0)
    g0 = pl.program_id(2) * eg

    def per_expert(el, carry):
        base = ((bi * N_EXPERTS) + g0 + el) * cap

        def body(jo, c2):
            j0 = pl.multiple_of(jo * UNROLL, UNROLL)
            for u in range(UNROLL):
                t = idx_ref[base + j0 + u]
                buf[pl.ds(j0 + u, 1), :] = h_ref[0, pl.ds(t, 1), :]
            return c2

        lax.fori_loop(0, cap // UNROLL, body, 0)
        xg_ref[el, 0] = buf[...].astype(BF16)
        return carry

    lax.fori_loop(0, eg, per_expert, 0)


def _dispatch(idx, h2, cap, eg):
    b, n, _ = h2.shape
    grid_spec = pltpu.PrefetchScalarGridSpec(
        num_scalar_prefetch=1,
        grid=(b, 2, N_EXPERTS // eg),
        in_specs=[pl.BlockSpec((1, n, D_HALF), lambda bi, dh, g, idx_r: (bi, 0, dh))],
        out_specs=pl.BlockSpec((eg, 1, cap, D_HALF), lambda bi, dh, g, idx_r: (g, bi, 0, dh)),
        scratch_shapes=[pltpu.VMEM((cap, D_HALF), F32)],
    )
    return pl.pallas_call(
        functools.partial(_dispatch_kernel, cap=cap, eg=eg),
        grid_spec=grid_spec,
        out_shape=jax.ShapeDtypeStruct((N_EXPERTS, b, cap, D_MODEL), BF16),
        compiler_params=_cparams(("arbitrary", "arbitrary", "arbitrary")),
        name="dispatch",
    )(idx.reshape(-1), h2)


def _combine_kernel(idx_ref, y_ref, acc_ref, *, cap, eg):
    bi = pl.program_id(0)
    g = pl.program_id(2)
    g0 = g * eg

    @pl.when(g == 0)
    def _():
        acc_ref[...] = jnp.zeros_like(acc_ref)

    def per_expert(el, carry):
        base = ((bi * N_EXPERTS) + g0 + el) * cap

        def body(jo, c2):
            j0 = pl.multiple_of(jo * UNROLL, UNROLL)
            toks = [idx_ref[base + j0 + u] for u in range(UNROLL)]
            olds = [acc_ref[0, pl.ds(t, 1), :] for t in toks]
            for u in range(UNROLL):
                acc_ref[0, pl.ds(toks[u], 1), :] = olds[u] + y_ref[el, 0, pl.ds(j0 + u, 1), :]
            return c2

        lax.fori_loop(0, cap // UNROLL, body, 0)
        return carry

    lax.fori_loop(0, eg, per_expert, 0)


def _combine(idx, y, n, cap, eg):
    b = y.shape[1]
    grid_spec = pltpu.PrefetchScalarGridSpec(
        num_scalar_prefetch=1,
        grid=(b, 2, N_EXPERTS // eg),
        in_specs=[pl.BlockSpec((eg, 1, cap, D_HALF), lambda bi, dh, g, idx_r: (g, bi, 0, dh))],
        out_specs=pl.BlockSpec((1, n, D_HALF), lambda bi, dh, g, idx_r: (bi, 0, dh)),
    )
    return pl.pallas_call(
        functools.partial(_combine_kernel, cap=cap, eg=eg),
        grid_spec=grid_spec,
        out_shape=jax.ShapeDtypeStruct((b, n, D_MODEL), F32),
        compiler_params=_cparams(("arbitrary", "arbitrary", "arbitrary")),
        name="combine",
    )(idx.reshape(-1), y)


def _ffn_kernel(xg_ref, gate_ref, wg_ref, wu_ref, wd_ref, y_ref, wg_s, wu_s, wd_s):
    @pl.when(pl.program_id(1) == 0)
    def _():
        wg_s[...] = wg_ref[0].astype(BF16)
        wu_s[...] = wu_ref[0].astype(BF16)
        wd_s[...] = wd_ref[0].astype(BF16)

    x = xg_ref[0]
    a = _dot(x, wg_s[...])
    u = _dot(x, wu_s[...])
    hmid = (a * (1.0 / (1.0 + jnp.exp(-a))) * u).astype(BF16)
    y_ref[0] = _dot(hmid, wd_s[...]) * gate_ref[0]


def _ffn(xg, gate, w_gate, w_up, w_down):
    _, r, _ = xg.shape
    tm = min(FFN_TM, r)
    w_spec = pl.BlockSpec((1, D_MODEL, D_MODEL), lambda e, m: (e, 0, 0))
    return pl.pallas_call(
        _ffn_kernel,
        grid=(N_EXPERTS, r // tm),
        in_specs=[
            pl.BlockSpec((1, tm, D_MODEL), lambda e, m: (e, m, 0)),
            pl.BlockSpec((1, tm, 1), lambda e, m: (e, m, 0)),
            w_spec, w_spec, w_spec,
        ],
        out_specs=pl.BlockSpec((1, tm, D_MODEL), lambda e, m: (e, m, 0)),
        out_shape=jax.ShapeDtypeStruct((N_EXPERTS, r, D_MODEL), F32),
        scratch_shapes=[pltpu.VMEM((D_MODEL, D_MODEL), BF16)] * 3,
        compiler_params=_cparams(("arbitrary", "arbitrary")),
        name="expert_ffn",
    )(xg, gate, w_gate, w_up, w_down)


def _moe(h2, affT, w_gate, w_up, w_down, eg):
    b, n, _ = h2.shape
    cap = (CAPACITY_FACTOR * n) // N_EXPERTS
    idx, gate = _select(affT, cap)
    xg = _dispatch(idx, h2, cap, eg)
    gate_rows = jnp.swapaxes(gate, 0, 1).reshape(N_EXPERTS, b * cap, 1)
    y = _ffn(xg.reshape(N_EXPERTS, b * cap, D_MODEL), gate_rows, w_gate, w_up, w_down)
    return _combine(idx, y.reshape(N_EXPERTS, b, cap, D_MODEL), n, cap, eg)


def _residual_kernel(x_ref, acc_ref, mod_ref, out_ref):
    out_ref[0] = x_ref[0] + mod_ref[0, 5:6, :] * acc_ref[0]


def _residual(x, acc, mod, mod_row):
    b, n, _ = x.shape
    tok_spec = pl.BlockSpec((1, TOK, D_MODEL), lambda bi, t: (bi, t, 0))
    return pl.pallas_call(
        _residual_kernel,
        grid=(b, n // TOK),
        in_specs=[tok_spec, tok_spec, pl.BlockSpec((1, 6, D_MODEL), lambda bi, t: (mod_row(bi), 0, 0))],
        out_specs=tok_spec,
        out_shape=jax.ShapeDtypeStruct((b, n, D_MODEL), F32),
        compiler_params=_cparams(("arbitrary", "arbitrary")),
        name="ffn_residual",
    )(x, acc, mod)


def _rope_tables(n):
    t = jnp.arange(n)
    inv_freq = ROPE_THETA ** (-jnp.arange(AXIS_PAIRS, dtype=F32) / AXIS_PAIRS)
    ang_r = (t // GRID_W).astype(F32)[None, :] * inv_freq[:, None]
    ang_c = (t % GRID_W).astype(F32)[None, :] * inv_freq[:, None]
    ang = jnp.concatenate([ang_r, ang_c], axis=0)
    return jnp.cos(ang), jnp.sin(ang)


def _band_chunk_starts(nt):
    return (0, None, nt - 3)


def _window_bias(nt):
    u = jnp.arange(TOK)[:, None]
    q = jnp.arange(TOK)[None, :]
    variants = []
    for rel0 in (0, -1, -2):
        slots = []
        for slot in range(3):
            d = (rel0 + slot) * TOK + u - q
            slots.append(jnp.where(jnp.abs(d) <= WINDOW, 0.0, NEG_INF).astype(F32))
        variants.append(jnp.stack(slots))
    return jnp.stack(variants)[:, None]


def _neighborhood_bias(rpb, rows):
    nt = rows // ROWS_PER_TOK
    kr_l = jnp.arange(TOK) // GRID_W
    kc = jnp.arange(TOK) % GRID_W
    variants = []
    for j, chunk0 in ((0, 0), (1, 0), (nt - 1, nt - 3)):
        qr = j * ROWS_PER_TOK + kr_l
        qc = kc
        rs = jnp.clip(qr - KH // 2, 0, rows - KH)
        cs = jnp.clip(qc - KW // 2, 0, GRID_W - KW)
        slots = []
        for slot in range(3):
            kr = (chunk0 + slot) * ROWS_PER_TOK + kr_l
            ok = ((kr[:, None] >= rs[None, :]) & (kr[:, None] < rs[None, :] + KH)
                  & (kc[:, None] >= cs[None, :]) & (kc[:, None] < cs[None, :] + KW))
            dr = jnp.clip(kr[:, None] - qr[None, :] + (KH - 1), 0, 2 * KH - 2)
            dc = jnp.clip(kc[:, None] - qc[None, :] + (KW - 1), 0, 2 * KW - 2)
            vals = rpb[:, dr, dc].astype(F32)
            slots.append(jnp.where(ok[None], vals, NEG_INF))
        variants.append(jnp.stack(slots, axis=1))
    return jnp.stack(variants)


def kernel(x_prompt, x_sample, cache_k, cache_v, c, c_ctx, w_ada, b_ada, norm_attn_g, norm_ffn_g,
           w_qkv, q_norm_g, k_norm_g, w_o, sink_logits, rel_pos_bias, w_router, w_gate, w_up, w_down):
    bp, n_p, _ = x_prompt.shape
    bs, n_s, _ = x_sample.shape
    past = cache_k.shape[2]
    assert n_p % TOK == 0 and n_s % TOK == 0 and past % TOK == 0
    assert n_s // TOK >= 3 and 1 + bs <= COND_ROWS

    cond = jnp.zeros((COND_ROWS, D_MODEL), F32).at[0].set(c_ctx).at[1:1 + bs].set(c)
    mods = _ada_modulation(cond, w_ada, b_ada).reshape(DEPTH, COND_ROWS, 6, D_MODEL)
    row_p = lambda bi: 0
    row_s = lambda bi: bi + 1

    nc = past // TOK
    kc_all = jnp.transpose(cache_k, (1, 0, 3, 2, 4)).astype(BF16)
    kc_all = kc_all.reshape(DEPTH, bs, N_KV_HEADS, nc, TOK, HEAD_DIM)
    vc_all = jnp.transpose(cache_v, (1, 0, 3, 4, 2)).astype(BF16)
    vc_all = jnp.transpose(vc_all.reshape(DEPTH, bs, N_KV_HEADS, HEAD_DIM, nc, TOK), (0, 1, 2, 4, 3, 5))

    rope_tabs = _rope_tables(n_s)
    win_bias = _window_bias(n_s // TOK)

    xp, xs = x_prompt, x_sample
    acc_p = acc_s = None
    new_k, new_v = [], []
    for i in range(DEPTH):
        kind = i % 3
        sink = sink_logits[i // 3] if kind == 1 else None
        prev = mods[i - 1] if i > 0 else None

        xp, qT, k, vT, ck, cv = _pre_attn(xp, acc_p, prev, mods[i], row_p, norm_attn_g[i], w_qkv[i],
                                          q_norm_g[i], k_norm_g[i], None, True)
        new_k.append(ck.reshape(bp, n_p, N_KV_HEADS, HEAD_DIM))
        new_v.append(cv.reshape(bp, n_p, N_KV_HEADS, HEAD_DIM))
        oT = _attention(qT, k, vT, None, sink, None)
        xp, h2, affT = _post_attn(oT, xp, mods[i], row_p, norm_ffn_g[i], w_o[i], w_router[i])
        acc_p = _moe(h2, affT, w_gate[i], w_up[i], w_down[i], N_EXPERTS)

        tabs = rope_tabs if kind != 2 else None
        xs, qT, k, vT = _pre_attn(xs, acc_s, prev, mods[i], row_s, norm_attn_g[i], w_qkv[i],
                                  q_norm_g[i], k_norm_g[i], tabs, False)
        ctx = (kc_all[i], vc_all[i])
        if kind == 0:
            oT = _attention(qT, k, vT, ctx, None, None)
        elif kind == 1:
            oT = _attention(qT, k, vT, ctx, sink, win_bias)
        else:
            oT = _attention(qT, k, vT, ctx, None, _neighborhood_bias(rel_pos_bias[i // 3], n_s // GRID_W))
        xs, h2, affT = _post_attn(oT, xs, mods[i], row_s, norm_ffn_g[i], w_o[i], w_router[i])
        acc_s = _moe(h2, affT, w_gate[i], w_up[i], w_down[i], 1)

    y_prompt = _residual(xp, acc_p, mods[DEPTH - 1], row_p)
    y_sample = _residual(xs, acc_s, mods[DEPTH - 1], row_s)
    return (y_prompt, y_sample, jnp.stack(new_k, axis=1), jnp.stack(new_v, axis=1))
```

```python
import functools

import jax
import jax.numpy as jnp
from jax import lax
from jax.experimental import pallas as pl
from jax.experimental.pallas import tpu as pltpu

F32 = jnp.float32
BF16 = jnp.bfloat16
I32 = jnp.int32

D_MODEL = 1024
DEPTH = 4
GRID_W = 64
N_HEADS = 16
N_KV_HEADS = 4
GROUP = N_HEADS // N_KV_HEADS
HEAD_DIM = 64
AXIS_PAIRS = HEAD_DIM // 4
ROPE_THETA = 10000.0
QKV_DIM = (N_HEADS + 2 * N_KV_HEADS) * HEAD_DIM
Q_DIM = N_HEADS * HEAD_DIM
KV_DIM = N_KV_HEADS * HEAD_DIM
WINDOW = 128
KH = 8
KW = 16
N_EXPERTS = 16
CAPACITY_FACTOR = 2
SCALE = HEAD_DIM ** -0.5
LOG2E = 1.4426950408889634
NEG_INF = -1e30
EPS = 1e-6

LANES = 128
TOK = 256
VT_ROWS = 64 + 16
ROWS_PER_TOK = TOK // GRID_W
COND_ROWS = 8
FFN_TM = 512
UNROLL = 8
BISECT_STEPS = 64
VMEM_LIMIT = 56 * 1024 * 1024


def _cparams(sem):
    return pltpu.CompilerParams(dimension_semantics=sem, vmem_limit_bytes=VMEM_LIMIT)


def _split3(x):
    hi = x.astype(BF16)
    r = x - hi.astype(F32)
    mid = r.astype(BF16)
    lo = (r - mid.astype(F32)).astype(BF16)
    return hi, mid, lo


def _dot(a, b, dims=(((1,), (0,)), ((), ()))):
    return lax.dot_general(a, b, dims, preferred_element_type=F32)


_NT = (((1,), (1,)), ((), ()))
_TN = (((0,), (0,)), ((), ()))


def _dot_exact_rhs(onehot_bf16, x_f32, dims=(((1,), (0,)), ((), ()))):
    hi, mid, lo = _split3(x_f32)
    return _dot(onehot_bf16, hi, dims) + _dot(onehot_bf16, mid, dims) + _dot(onehot_bf16, lo, dims)


def _dot_f32(a, b, dims):
    a_hi, a_mid, _ = _split3(a)
    b_hi, b_mid, _ = _split3(b)
    return (_dot(a_hi, b_hi, dims) + _dot(a_hi, b_mid, dims) + _dot(a_mid, b_hi, dims))


def _mod_rmsnorm(x, g, shift, scale):
    ms = jnp.mean(x * x, axis=-1, keepdims=True)
    return (x * lax.rsqrt(ms + EPS) * g) * (1.0 + scale) + shift


ADA_TN = 1536


def _ada_kernel(cond_ref, w_ref, b_ref, out_ref):
    cnd = cond_ref[...]
    act = cnd * (1.0 / (1.0 + jnp.exp(-cnd)))
    out_ref[0] = _dot_f32(act, w_ref[0], (((1,), (0,)), ((), ()))) + b_ref[0]


def _ada_modulation(cond, w_ada, b_ada):
    n_out = 6 * D_MODEL
    return pl.pallas_call(
        _ada_kernel,
        grid=(DEPTH, n_out // ADA_TN),
        in_specs=[
            pl.BlockSpec((COND_ROWS, D_MODEL), lambda i, n: (0, 0)),
            pl.BlockSpec((1, D_MODEL, ADA_TN), lambda i, n: (i, 0, n)),
            pl.BlockSpec((1, 1, ADA_TN), lambda i, n: (i, 0, n)),
        ],
        out_specs=pl.BlockSpec((1, COND_ROWS, ADA_TN), lambda i, n: (i, 0, n)),
        out_shape=jax.ShapeDtypeStruct((DEPTH, COND_ROWS, n_out), F32),
        compiler_params=_cparams(("arbitrary", "arbitrary")),
        name="ada_modulation",
    )(cond, w_ada, b_ada.reshape(DEPTH, 1, n_out))


def _pre_attn_kernel(*refs, fuse_residual, rope, emit_cache):
    it = iter(refs)
    x_ref = next(it)
    if fuse_residual:
        acc_ref = next(it)
        pmod_ref = next(it)
    mod_ref = next(it)
    g_ref = next(it)
    w_ref = next(it)
    gq_ref = next(it)
    gk_ref = next(it)
    if rope:
        cos_ref = next(it)
        sin_ref = next(it)
    if fuse_residual:
        xo_ref = next(it)
    qT_ref = next(it)
    k_ref = next(it)
    vT_ref = next(it)
    if emit_cache:
        ck_ref = next(it)
        cv_ref = next(it)
    w_s = next(it)

    @pl.when((pl.program_id(0) == 0) & (pl.program_id(1) == 0))
    def _():
        w_s[...] = w_ref[...].astype(BF16)

    x = x_ref[0]
    if fuse_residual:
        x = x + pmod_ref[0, 5:6, :] * acc_ref[0]
        xo_ref[0] = x
    h = _mod_rmsnorm(x, g_ref[...], mod_ref[0, 0:1, :], mod_ref[0, 1:2, :])
    qkv = _dot(h.astype(BF16), w_s[...])
    if emit_cache:
        cv_ref[0] = qkv[:, Q_DIM + KV_DIM:]
    qkvT = qkv.T

    def head_norm(blk, gain):
        ss = jnp.sum(blk * blk, axis=0, keepdims=True)
        return blk * lax.rsqrt(ss * (1.0 / HEAD_DIM) + EPS) * gain

    def rotate(blk):
        if not rope:
            return blk
        cs = cos_ref[...]
        sn = sin_ref[...]
        parts = []
        for a in range(2):
            x1 = blk[32 * a:32 * a + 16]
            x2 = blk[32 * a + 16:32 * a + 32]
            c = cs[16 * a:16 * a + 16]
            s = sn[16 * a:16 * a + 16]
            parts.append(x1 * c - x2 * s)
            parts.append(x2 * c + x1 * s)
        return jnp.concatenate(parts, axis=0)

    gq = gq_ref[...]
    gk = gk_ref[...]
    for hd in range(N_HEADS):
        blk = qkvT[HEAD_DIM * hd:HEAD_DIM * (hd + 1)]
        qT_ref[0, hd] = (rotate(head_norm(blk, gq)) * (SCALE * LOG2E)).astype(BF16)
    kn_all = []
    for kv in range(N_KV_HEADS):
        off = Q_DIM + HEAD_DIM * kv
        kn_all.append(rotate(head_norm(qkvT[off:off + HEAD_DIM], gk)))
        offv = Q_DIM + KV_DIM + HEAD_DIM * kv
        vT_ref[0, kv, 0, :HEAD_DIM] = qkvT[offv:offv + HEAD_DIM].astype(BF16)
        vT_ref[0, kv, 0, HEAD_DIM:] = jnp.ones((VT_ROWS - HEAD_DIM, TOK), BF16)
    kn_tok = jnp.concatenate(kn_all, axis=0).T
    for kv in range(N_KV_HEADS):
        k_ref[0, kv, 0] = kn_tok[:, HEAD_DIM * kv:HEAD_DIM * (kv + 1)].astype(BF16)
    if emit_cache:
        ck_ref[0] = kn_tok


def _pre_attn(x, acc, prev_mod, mod, mod_row, g, w_qkv, gq, gk, rope_tabs, emit_cache):
    b, n, _ = x.shape
    nt = n // TOK
    fuse = acc is not None
    rope = rope_tabs is not None
    tok_spec = pl.BlockSpec((1, TOK, D_MODEL), lambda bi, t: (bi, t, 0))
    mod_spec = pl.BlockSpec((1, 6, D_MODEL), lambda bi, t: (mod_row(bi), 0, 0))
    ins, in_specs = [x], [tok_spec]
    if fuse:
        ins += [acc, prev_mod]
        in_specs += [tok_spec, mod_spec]
    ins += [mod, g.reshape(1, D_MODEL), w_qkv,
            jnp.broadcast_to(gq[:, None], (HEAD_DIM, TOK)),
            jnp.broadcast_to(gk[:, None], (HEAD_DIM, TOK))]
    in_specs += [
        mod_spec,
        pl.BlockSpec((1, D_MODEL), lambda bi, t: (0, 0)),
        pl.BlockSpec((D_MODEL, QKV_DIM), lambda bi, t: (0, 0)),
        pl.BlockSpec((HEAD_DIM, TOK), lambda bi, t: (0, 0)),
        pl.BlockSpec((HEAD_DIM, TOK), lambda bi, t: (0, 0)),
    ]
    if rope:
        ins += list(rope_tabs)
        in_specs += [pl.BlockSpec((2 * AXIS_PAIRS, TOK), lambda bi, t: (0, t))] * 2
    out_shape, out_specs = [], []
    if fuse:
        out_shape.append(jax.ShapeDtypeStruct((b, n, D_MODEL), F32))
        out_specs.append(tok_spec)
    out_shape += [
        jax.ShapeDtypeStruct((b, N_HEADS, HEAD_DIM, n), BF16),
        jax.ShapeDtypeStruct((b, N_KV_HEADS, nt, TOK, HEAD_DIM), BF16),
        jax.ShapeDtypeStruct((b, N_KV_HEADS, nt, VT_ROWS, TOK), BF16),
    ]
    out_specs += [
        pl.BlockSpec((1, N_HEADS, HEAD_DIM, TOK), lambda bi, t: (bi, 0, 0, t)),
        pl.BlockSpec((1, N_KV_HEADS, 1, TOK, HEAD_DIM), lambda bi, t: (bi, 0, t, 0, 0)),
        pl.BlockSpec((1, N_KV_HEADS, 1, VT_ROWS, TOK), lambda bi, t: (bi, 0, t, 0, 0)),
    ]
    if emit_cache:
        out_shape += [jax.ShapeDtypeStruct((b, n, KV_DIM), F32)] * 2
        out_specs += [pl.BlockSpec((1, TOK, KV_DIM), lambda bi, t: (bi, t, 0))] * 2
    outs = pl.pallas_call(
        functools.partial(_pre_attn_kernel, fuse_residual=fuse, rope=rope, emit_cache=emit_cache),
        grid=(b, nt),
        in_specs=in_specs,
        out_specs=out_specs,
        out_shape=out_shape,
        scratch_shapes=[pltpu.VMEM((D_MODEL, QKV_DIM), BF16)],
        compiler_params=_cparams(("arbitrary", "arbitrary")),
        name="pre_attn",
    )(*ins)
    outs = list(outs)
    x_new = outs.pop(0) if fuse else x
    return [x_new] + outs


def _attn_kernel(*refs, mode, n_local, has_ctx, has_sink, bias_heads):
    it = iter(refs)
    qT_ref = next(it)
    if mode == "dense":
        k_ref = next(it)
        vT_ref = next(it)
    else:
        k_refs = [next(it) for _ in range(3)]
        vT_refs = [next(it) for _ in range(3)]
        bias_ref = next(it)
    if has_ctx:
        kc_ref = next(it)
        vcT_ref = next(it)
    if has_sink:
        sink_ref = next(it)
    oT_ref = next(it)
    m_s = next(it)
    acc_s = next(it)
    s_buf = next(it)

    kv = pl.program_id(1)
    for hd in range(GROUP):
        acc_s[hd, :HEAD_DIM] = jnp.zeros((HEAD_DIM, TOK), F32)
        if has_sink:
            m_s[hd] = jnp.full((1, TOK), sink_ref[kv * GROUP + hd] * LOG2E, F32)
            acc_s[hd, HEAD_DIM:] = jnp.ones((VT_ROWS - HEAD_DIM, TOK), F32)
        else:
            m_s[hd] = jnp.full((1, TOK), NEG_INF, F32)
            acc_s[hd, HEAD_DIM:] = jnp.zeros((VT_ROWS - HEAD_DIM, TOK), F32)

    def scores(par, kch):
        for hd in range(GROUP):
            s_buf[par, hd] = _dot(kch, qT_ref[0, hd])

    def consume(par, vch, bias_of_head):
        for hd in range(GROUP):
            s = s_buf[par, hd]
            if bias_of_head is not None:
                s = s + bias_of_head(hd)
            m_prev = m_s[hd]
            m_new = jnp.maximum(m_prev, jnp.max(s, axis=0, keepdims=True))
            alpha = jnp.exp2(m_prev - m_new)
            p = jnp.exp2(s - m_new)
            acc_s[hd] = alpha * acc_s[hd] + _dot(vch, p.astype(BF16))
            m_s[hd] = m_new

    static_chunks = []
    if mode == "band":
        for slot in range(3):
            if bias_heads == 1:
                bias_fn = lambda hd, slot=slot: bias_ref[0, 0, slot]
            else:
                bias_fn = lambda hd, slot=slot: bias_ref[0, hd, slot]
            static_chunks.append((k_refs[slot].at[0, 0, 0], vT_refs[slot].at[0, 0, 0], bias_fn))
    if has_ctx:
        for c in range(kc_ref.shape[2]):
            static_chunks.append((kc_ref.at[0, 0, c], vcT_ref.at[0, 0, c], None))

    if mode == "dense":
        n_tail = 2 if n_local % 2 == 0 else 1
        local_tail = [(k_ref.at[0, 0, c], vT_ref.at[0, 0, c], None) for c in range(n_local - n_tail, n_local)]
        static_chunks = local_tail + static_chunks
        if n_local > n_tail:
            scores(0, k_ref[0, 0, 0])

            def body(i, carry):
                c = 2 * i
                scores(1, k_ref[0, 0, c + 1])
                consume(0, vT_ref[0, 0, c], None)
                scores(0, k_ref[0, 0, c + 2])
                consume(1, vT_ref[0, 0, c + 1], None)
                return carry
            lax.fori_loop(0, (n_local - n_tail) // 2, body, 0)
        else:
            scores(0, static_chunks[0][0][...])
    else:
        scores(0, static_chunks[0][0][...])
    for i, (_, v_at, bias_fn) in enumerate(static_chunks):
        if i + 1 < len(static_chunks):
            scores((i + 1) % 2, static_chunks[i + 1][0][...])
        consume(i % 2, v_at[...], bias_fn)

    for hd in range(GROUP):
        oT_ref[0, hd] = (acc_s[hd, :HEAD_DIM] / acc_s[hd, HEAD_DIM:HEAD_DIM + 1]).astype(BF16)


def _attention(qT, k, vT, ctx, sink, bias):
    b, _, _, n = qT.shape
    nt = n // TOK
    mode = "dense" if bias is None else "band"
    ins = [qT]
    in_specs = [pl.BlockSpec((1, GROUP, HEAD_DIM, TOK), lambda bi, kv, j: (bi, kv, 0, j))]
    if mode == "dense":
        ins += [k, vT]
        in_specs += [
            pl.BlockSpec((1, 1, nt, TOK, HEAD_DIM), lambda bi, kv, j: (bi, kv, 0, 0, 0)),
            pl.BlockSpec((1, 1, nt, VT_ROWS, TOK), lambda bi, kv, j: (bi, kv, 0, 0, 0)),
        ]
        bias_heads = 0
    else:
        def base(j):
            return jnp.clip(j - 1, 0, nt - 3)
        ins += [k, k, k, vT, vT, vT, bias]
        for slot in range(3):
            in_specs.append(pl.BlockSpec(
                (1, 1, 1, TOK, HEAD_DIM),
                lambda bi, kv, j, slot=slot: (bi, kv, base(j) + slot, 0, 0)))
        for slot in range(3):
            in_specs.append(pl.BlockSpec(
                (1, 1, 1, VT_ROWS, TOK),
                lambda bi, kv, j, slot=slot: (bi, kv, base(j) + slot, 0, 0)))
        bias_heads = 1 if bias.shape[1] == 1 else GROUP

        def variant(j):
            return (j > 0).astype(I32) + (j == nt - 1).astype(I32)
        if bias_heads == 1:
            in_specs.append(pl.BlockSpec((1, 1, 3, TOK, TOK), lambda bi, kv, j: (variant(j), 0, 0, 0, 0)))
        else:
            in_specs.append(pl.BlockSpec((1, GROUP, 3, TOK, TOK), lambda bi, kv, j: (variant(j), kv, 0, 0, 0)))
    if ctx is not None:
        kc, vcT = ctx
        nc = kc.shape[2]
        ins += [kc, vcT]
        in_specs += [
            pl.BlockSpec((1, 1, nc, TOK, HEAD_DIM), lambda bi, kv, j: (bi, kv, 0, 0, 0)),
            pl.BlockSpec((1, 1, nc, VT_ROWS, TOK), lambda bi, kv, j: (bi, kv, 0, 0, 0)),
        ]
    if sink is not None:
        ins.append(sink)
        in_specs.append(pl.BlockSpec(memory_space=pltpu.SMEM))
    return pl.pallas_call(
        functools.partial(_attn_kernel, mode=mode, n_local=nt, has_ctx=ctx is not None,
                          has_sink=sink is not None, bias_heads=bias_heads),
        grid=(b, N_KV_HEADS, nt),
        in_specs=in_specs,
        out_specs=pl.BlockSpec((1, GROUP, HEAD_DIM, TOK), lambda bi, kv, j: (bi, kv, 0, j)),
        out_shape=jax.ShapeDtypeStruct((b, N_HEADS, HEAD_DIM, n), BF16),
        scratch_shapes=[
            pltpu.VMEM((GROUP, 1, TOK), F32),
            pltpu.VMEM((GROUP, VT_ROWS, TOK), F32),
            pltpu.VMEM((2, GROUP, TOK, TOK), F32),
        ],
        compiler_params=_cparams(("arbitrary", "arbitrary", "arbitrary")),
        name="attention_" + mode,
    )(*ins)


def _post_attn_kernel(oT_ref, x_ref, mod_ref, g_ref, wo_ref, wrT_ref, xm_ref, h2_ref, affT_ref, wo_s):
    @pl.when((pl.program_id(0) == 0) & (pl.program_id(1) == 0))
    def _():
        wo_s[...] = wo_ref[...].astype(BF16)

    oT = oT_ref[0].reshape(Q_DIM, TOK)
    y = _dot(oT, wo_s[...], _TN)
    xm = x_ref[0] + mod_ref[0, 2:3, :] * y
    xm_ref[0] = xm
    h2 = _mod_rmsnorm(xm, g_ref[...], mod_ref[0, 3:4, :], mod_ref[0, 4:5, :])
    h2_ref[0] = h2
    logitsT = _dot_f32(wrT_ref[...], h2, _NT)
    mx = jnp.max(logitsT, axis=0, keepdims=True)
    e = jnp.exp(logitsT - mx)
    affT_ref[0] = e / jnp.sum(e, axis=0, keepdims=True)


def _post_attn(oT, x, mod, mod_row, g, w_o, w_router):
    b, n, _ = x.shape
    nt = n // TOK
    tok_spec = pl.BlockSpec((1, TOK, D_MODEL), lambda bi, t: (bi, t, 0))
    return pl.pallas_call(
        _post_attn_kernel,
        grid=(b, nt),
        in_specs=[
            pl.BlockSpec((1, N_HEADS, HEAD_DIM, TOK), lambda bi, t: (bi, 0, 0, t)),
            tok_spec,
            pl.BlockSpec((1, 6, D_MODEL), lambda bi, t: (mod_row(bi), 0, 0)),
            pl.BlockSpec((1, D_MODEL), lambda bi, t: (0, 0)),
            pl.BlockSpec((Q_DIM, D_MODEL), lambda bi, t: (0, 0)),
            pl.BlockSpec((N_EXPERTS, D_MODEL), lambda bi, t: (0, 0)),
        ],
        out_specs=[tok_spec, tok_spec, pl.BlockSpec((1, N_EXPERTS, TOK), lambda bi, t: (bi, 0, t))],
        out_shape=[
            jax.ShapeDtypeStruct((b, n, D_MODEL), F32),
            jax.ShapeDtypeStruct((b, n, D_MODEL), F32),
            jax.ShapeDtypeStruct((b, N_EXPERTS, n), F32),
        ],
        scratch_shapes=[pltpu.VMEM((Q_DIM, D_MODEL), BF16)],
        compiler_params=_cparams(("arbitrary", "arbitrary")),
        name="post_attn",
    )(oT, x, mod, g.reshape(1, D_MODEL), w_o, w_router.T)


def _select_kernel(ag_ref, a_ref, idx_ref, gate_ref, *, groups, chunks, cap):
    rows = groups * chunks
    slots = max(cap, LANES)
    a = a_ref[0]

    ri = lax.broadcasted_iota(I32, (rows, rows), 0)
    ci = lax.broadcasted_iota(I32, (rows, rows), 1)
    same = (ri // chunks) == (ci // chunks)
    blockdiag = jnp.where(same, 1.0, 0.0).astype(BF16)
    lower = jnp.where(same & (ci < ri), 1.0, 0.0).astype(BF16)
    li = lax.broadcasted_iota(I32, (LANES, LANES), 0)
    lj = lax.broadcasted_iota(I32, (LANES, LANES), 1)
    upper_incl = jnp.where(li <= lj, 1.0, 0.0).astype(BF16)
    ones_ll = jnp.ones((LANES, LANES), BF16)

    def as_bf16(mask):
        return jnp.where(mask, 1.0, 0.0).astype(BF16)

    def row_total(x01):
        return _dot(x01, ones_ll)

    def expert_total(x01):
        return _dot(blockdiag, row_total(x01).astype(BF16))

    capf = float(cap)

    ag = ag_ref[0]

    def search(step, bounds):
        lo, hi = bounds
        mid = 0.5 * (lo + hi)
        ok = jnp.sum(jnp.where(ag >= mid, 1.0, 0.0), axis=1, keepdims=True) >= capf
        return jnp.where(ok, mid, lo), jnp.where(ok, hi, mid)

    lo_g, hi_g = lax.fori_loop(0, BISECT_STEPS, search,
                               (jnp.zeros((groups, 1), F32), jnp.full((groups, 1), 2.0, F32)))
    to_rows = as_bf16(lax.broadcasted_iota(I32, (rows, groups), 0) // chunks
                      == lax.broadcasted_iota(I32, (rows, groups), 1))
    lo = _dot_exact_rhs(to_rows, jnp.broadcast_to(lo_g, (groups, LANES)))
    hi = _dot_exact_rhs(to_rows, jnp.broadcast_to(hi_g, (groups, LANES)))

    gt = a >= hi
    eq = (a >= lo) & (a < hi)
    need = capf - expert_total(as_bf16(gt))
    eq01 = as_bf16(eq)
    eq_rank = (_dot(eq01, upper_incl) - eq01.astype(F32)
               + _dot(lower, row_total(eq01).astype(BF16)))
    sel = gt | (eq & (eq_rank < need))
    sel01 = as_bf16(sel)
    sel_rows = row_total(sel01)
    c_excl = _dot(lower, sel_rows.astype(BF16))
    c_incl = c_excl + sel_rows
    key = 2.0 * (_dot(sel01, upper_incl) + c_excl) - sel01.astype(F32)
    first_lane = as_bf16(lax.broadcasted_iota(I32, (8, LANES), 1) == 0)
    c_excl_l = _dot_exact_rhs(first_lane, c_excl, _NT)[0:1, :]
    c_incl_l = _dot_exact_rhs(first_lane, c_incl, _NT)[0:1, :]

    jcol = lax.broadcasted_iota(I32, (slots, rows), 0).astype(F32)
    rlane = lax.broadcasted_iota(I32, (slots, rows), 1)
    jl = lax.broadcasted_iota(I32, (slots, LANES), 0).astype(F32)
    lane8 = lax.broadcasted_iota(I32, (8, LANES), 1).astype(BF16)
    chunk8 = (lax.broadcasted_iota(I32, (8, rows), 1) % chunks).astype(BF16)
    ones8 = jnp.ones((8, LANES), BF16)

    def per_group(e, carry):
        mine = (rlane // chunks) == e
        onehot = as_bf16(mine & (c_excl_l <= jcol) & (jcol < c_incl_l))
        hit = _dot_exact_rhs(onehot, key) == 2.0 * jl + 1.0
        hit01 = as_bf16(hit)
        tok = _dot(chunk8, onehot, _NT) * float(LANES) + _dot(lane8, hit01, _NT)
        gsel = jnp.where(hit, _dot_exact_rhs(onehot, a), 0.0)
        g = _dot_exact_rhs(ones8, gsel, _NT)
        idx_ref[0, pl.ds(e, 1), :] = tok[0:1, :cap].astype(I32)
        gate_ref[0, pl.ds(e, 1), :] = g[0:1, :cap]
        return carry

    lax.fori_loop(0, groups, per_group, 0)


def _select(affT, cap):
    b, _, n = affT.shape
    chunks = n // LANES
    bstep = b if b * N_EXPERTS * chunks <= 512 else 1
    groups = bstep * N_EXPERTS
    rows = groups * chunks
    idx, gate = pl.pallas_call(
        functools.partial(_select_kernel, groups=groups, chunks=chunks, cap=cap),
        grid=(b // bstep,),
        in_specs=[pl.BlockSpec((1, groups, n), lambda bi: (bi, 0, 0)),
                  pl.BlockSpec((1, rows, LANES), lambda bi: (bi, 0, 0))],
        out_specs=[pl.BlockSpec((1, groups, cap), lambda bi: (bi, 0, 0))] * 2,
        out_shape=[jax.ShapeDtypeStruct((b // bstep, groups, cap), I32),
                   jax.ShapeDtypeStruct((b // bstep, groups, cap), F32)],
        compiler_params=_cparams(("arbitrary",)),
        name="select",
    )(affT.reshape(b // bstep, groups, n), affT.reshape(b // bstep, rows, LANES))
    return idx.reshape(b, N_EXPERTS, cap), gate.reshape(b, N_EXPERTS, cap)


def _dispatch_kernel(idx_ref, h_ref, xg_ref, buf, *, cap, eg):
    bi = pl.program_id(0)
    g0 = pl.program_id(1) * eg

    def per_expert(el, carry):
        base = ((bi * N_EXPERTS) + g0 + el) * cap

        def body(jo, c2):
            j0 = pl.multiple_of(jo * UNROLL, UNROLL)
            for u in range(UNROLL):
                t = idx_ref[base + j0 + u]
                buf[pl.ds(j0 + u, 1), :] = h_ref[0, pl.ds(t, 1), :]
            return c2

        lax.fori_loop(0, cap // UNROLL, body, 0)
        xg_ref[el, 0] = buf[...].astype(BF16)
        return carry

    lax.fori_loop(0, eg, per_expert, 0)


def _dispatch(idx, h2, cap, eg):
    b, n, _ = h2.shape
    grid_spec = pltpu.PrefetchScalarGridSpec(
        num_scalar_prefetch=1,
        grid=(b, N_EXPERTS // eg),
        in_specs=[pl.BlockSpec((1, n, D_MODEL), lambda bi, g, idx_r: (bi, 0, 0))],
        out_specs=pl.BlockSpec((eg, 1, cap, D_MODEL), lambda bi, g, idx_r: (g, bi, 0, 0)),
        scratch_shapes=[pltpu.VMEM((cap, D_MODEL), F32)],
    )
    return pl.pallas_call(
        functools.partial(_dispatch_kernel, cap=cap, eg=eg),
        grid_spec=grid_spec,
        out_shape=jax.ShapeDtypeStruct((N_EXPERTS, b, cap, D_MODEL), BF16),
        compiler_params=_cparams(("arbitrary", "arbitrary")),
        name="dispatch",
    )(idx.reshape(-1), h2)


def _combine_kernel(idx_ref, y_ref, acc_ref, *, cap, eg):
    bi = pl.program_id(0)
    g = pl.program_id(1)
    g0 = g * eg

    @pl.when(g == 0)
    def _():
        acc_ref[...] = jnp.zeros_like(acc_ref)

    def per_expert(el, carry):
        base = ((bi * N_EXPERTS) + g0 + el) * cap

        def body(jo, c2):
            j0 = pl.multiple_of(jo * UNROLL, UNROLL)
            toks = [idx_ref[base + j0 + u] for u in range(UNROLL)]
            olds = [acc_ref[0, pl.ds(t, 1), :] for t in toks]
            for u in range(UNROLL):
                acc_ref[0, pl.ds(toks[u], 1), :] = olds[u] + y_ref[el, 0, pl.ds(j0 + u, 1), :]
            return c2

        lax.fori_loop(0, cap // UNROLL, body, 0)
        return carry

    lax.fori_loop(0, eg, per_expert, 0)


def _combine(idx, y, n, cap, eg):
    b = y.shape[1]
    grid_spec = pltpu.PrefetchScalarGridSpec(
        num_scalar_prefetch=1,
        grid=(b, N_EXPERTS // eg),
        in_specs=[pl.BlockSpec((eg, 1, cap, D_MODEL), lambda bi, g, idx_r: (g, bi, 0, 0))],
        out_specs=pl.BlockSpec((1, n, D_MODEL), lambda bi, g, idx_r: (bi, 0, 0)),
    )
    return pl.pallas_call(
        functools.partial(_combine_kernel, cap=cap, eg=eg),
        grid_spec=grid_spec,
        out_shape=jax.ShapeDtypeStruct((b, n, D_MODEL), F32),
        compiler_params=_cparams(("arbitrary", "arbitrary")),
        name="combine",
    )(idx.reshape(-1), y)


def _ffn_kernel(xg_ref, gate_ref, wg_ref, wu_ref, wd_ref, y_ref, wg_s, wu_s, wd_s):
    @pl.when(pl.program_id(1) == 0)
    def _():
        wg_s[...] = wg_ref[0].astype(BF16)
        wu_s[...] = wu_ref[0].astype(BF16)
        wd_s[...] = wd_ref[0].astype(BF16)

    x = xg_ref[0]
    a = _dot(x, wg_s[...])
    u = _dot(x, wu_s[...])
    hmid = (a * (1.0 / (1.0 + jnp.exp(-a))) * u).astype(BF16)
    y_ref[0] = _dot(hmid, wd_s[...]) * gate_ref[0]


def _ffn(xg, gate, w_gate, w_up, w_down):
    _, r, _ = xg.shape
    tm = min(FFN_TM, r)
    w_spec = pl.BlockSpec((1, D_MODEL, D_MODEL), lambda e, m: (e, 0, 0))
    return pl.pallas_call(
        _ffn_kernel,
        grid=(N_EXPERTS, r // tm),
        in_specs=[
            pl.BlockSpec((1, tm, D_MODEL), lambda e, m: (e, m, 0)),
            pl.BlockSpec((1, tm, 1), lambda e, m: (e, m, 0)),
            w_spec, w_spec, w_spec,
        ],
        out_specs=pl.BlockSpec((1, tm, D_MODEL), lambda e, m: (e, m, 0)),
        out_shape=jax.ShapeDtypeStruct((N_EXPERTS, r, D_MODEL), F32),
        scratch_shapes=[pltpu.VMEM((D_MODEL, D_MODEL), BF16)] * 3,
        compiler_params=_cparams(("arbitrary", "arbitrary")),
        name="expert_ffn",
    )(xg, gate, w_gate, w_up, w_down)


def _moe(h2, affT, w_gate, w_up, w_down, eg):
    b, n, _ = h2.shape
    cap = (CAPACITY_FACTOR * n) // N_EXPERTS
    idx, gate = _select(affT, cap)
    xg = _dispatch(idx, h2, cap, eg)
    gate_rows = jnp.swapaxes(gate, 0, 1).reshape(N_EXPERTS, b * cap, 1)
    y = _ffn(xg.reshape(N_EXPERTS, b * cap, D_MODEL), gate_rows, w_gate, w_up, w_down)
    return _combine(idx, y.reshape(N_EXPERTS, b, cap, D_MODEL), n, cap, eg)


def _residual_kernel(x_ref, acc_ref, mod_ref, out_ref):
    out_ref[0] = x_ref[0] + mod_ref[0, 5:6, :] * acc_ref[0]


def _residual(x, acc, mod, mod_row):
    b, n, _ = x.shape
    tok_spec = pl.BlockSpec((1, TOK, D_MODEL), lambda bi, t: (bi, t, 0))
    return pl.pallas_call(
        _residual_kernel,
        grid=(b, n // TOK),
        in_specs=[tok_spec, tok_spec, pl.BlockSpec((1, 6, D_MODEL), lambda bi, t: (mod_row(bi), 0, 0))],
        out_specs=tok_spec,
        out_shape=jax.ShapeDtypeStruct((b, n, D_MODEL), F32),
        compiler_params=_cparams(("arbitrary", "arbitrary")),
        name="ffn_residual",
    )(x, acc, mod)


def _rope_tables(n):
    t = jnp.arange(n)
    inv_freq = ROPE_THETA ** (-jnp.arange(AXIS_PAIRS, dtype=F32) / AXIS_PAIRS)
    ang_r = (t // GRID_W).astype(F32)[None, :] * inv_freq[:, None]
    ang_c = (t % GRID_W).astype(F32)[None, :] * inv_freq[:, None]
    ang = jnp.concatenate([ang_r, ang_c], axis=0)
    return jnp.cos(ang), jnp.sin(ang)


def _band_chunk_starts(nt):
    return (0, None, nt - 3)


def _window_bias(nt):
    u = jnp.arange(TOK)[:, None]
    q = jnp.arange(TOK)[None, :]
    variants = []
    for rel0 in (0, -1, -2):
        slots = []
        for slot in range(3):
            d = (rel0 + slot) * TOK + u - q
            slots.append(jnp.where(jnp.abs(d) <= WINDOW, 0.0, NEG_INF).astype(F32))
        variants.append(jnp.stack(slots))
    return jnp.stack(variants)[:, None]


def _neighborhood_bias(rpb, rows):
    nt = rows // ROWS_PER_TOK
    kc = jnp.arange(GRID_W)[:, None]
    qc = jnp.arange(GRID_W)[None, :]
    cs = jnp.clip(qc - KW // 2, 0, GRID_W - KW)
    col_ok = (kc >= cs) & (kc < cs + KW)
    diag = (kc - qc + (KW - 1))[None] == jnp.arange(2 * KW - 1)[:, None, None]
    toep = jnp.einsum("hdj,jkq->hdkq", rpb.astype(F32) * LOG2E, diag.astype(F32),
                      precision=lax.Precision.HIGHEST)
    toep = jnp.where(col_ok, toep, NEG_INF)
    masked = jnp.full((rpb.shape[0], GRID_W, GRID_W), NEG_INF, F32)
    variants = []
    for j, chunk0 in ((0, 0), (1, 0), (nt - 1, nt - 3)):
        slots = []
        for slot in range(3):
            block_rows = []
            for a in range(ROWS_PER_TOK):
                kr = (chunk0 + slot) * ROWS_PER_TOK + a
                blocks = []
                for b in range(ROWS_PER_TOK):
                    qr = j * ROWS_PER_TOK + b
                    rs = min(max(qr - KH // 2, 0), rows - KH)
                    ok = rs <= kr < rs + KH
                    blocks.append(toep[:, kr - qr + (KH - 1)] if ok else masked)
                block_rows.append(jnp.concatenate(blocks, axis=-1))
            slots.append(jnp.concatenate(block_rows, axis=-2))
        variants.append(jnp.stack(slots, axis=1))
    return jnp.stack(variants)


def kernel(x_prompt, x_sample, cache_k, cache_v, c, c_ctx, w_ada, b_ada, norm_attn_g, norm_ffn_g,
           w_qkv, q_norm_g, k_norm_g, w_o, sink_logits, rel_pos_bias, w_router, w_gate, w_up, w_down):
    bp, n_p, _ = x_prompt.shape
    bs, n_s, _ = x_sample.shape
    past = cache_k.shape[2]
    assert n_p % TOK == 0 and n_s % TOK == 0 and past % TOK == 0
    assert n_s // TOK >= 3 and 1 + bs <= COND_ROWS

    cond = jnp.zeros((COND_ROWS, D_MODEL), F32).at[0].set(c_ctx).at[1:1 + bs].set(c)
    mods = _ada_modulation(cond, w_ada, b_ada).reshape(DEPTH, COND_ROWS, 6, D_MODEL)
    row_p = lambda bi: 0
    row_s = lambda bi: bi + 1

    nc = past // TOK
    kc_all = jnp.transpose(cache_k, (1, 0, 3, 2, 4)).astype(BF16)
    kc_all = kc_all.reshape(DEPTH, bs, N_KV_HEADS, nc, TOK, HEAD_DIM)
    vc_all = jnp.transpose(cache_v, (1, 0, 3, 4, 2)).astype(BF16)
    vc_all = jnp.transpose(vc_all.reshape(DEPTH, bs, N_KV_HEADS, HEAD_DIM, nc, TOK), (0, 1, 2, 4, 3, 5))
    vc_all = jnp.concatenate(
        [vc_all, jnp.ones((DEPTH, bs, N_KV_HEADS, nc, VT_ROWS - HEAD_DIM, TOK), BF16)], axis=4)

    rope_tabs = _rope_tables(n_s)
    win_bias = _window_bias(n_s // TOK)

    xp, xs = x_prompt, x_sample
    acc_p = acc_s = None
    new_k, new_v = [], []
    for i in range(DEPTH):
        kind = i % 3
        sink = sink_logits[i // 3] if kind == 1 else None
        prev = mods[i - 1] if i > 0 else None

        xp, qT, k, vT, ck, cv = _pre_attn(xp, acc_p, prev, mods[i], row_p, norm_attn_g[i], w_qkv[i],
                                          q_norm_g[i], k_norm_g[i], None, True)
        new_k.append(ck.reshape(bp, n_p, N_KV_HEADS, HEAD_DIM))
        new_v.append(cv.reshape(bp, n_p, N_KV_HEADS, HEAD_DIM))
        oT = _attention(qT, k, vT, None, sink, None)
        xp, h2, affT = _post_attn(oT, xp, mods[i], row_p, norm_ffn_g[i], w_o[i], w_router[i])
        acc_p = _moe(h2, affT, w_gate[i], w_up[i], w_down[i], N_EXPERTS)

        tabs = rope_tabs if kind != 2 else None
        xs, qT, k, vT = _pre_attn(xs, acc_s, prev, mods[i], row_s, norm_attn_g[i], w_qkv[i],
                                  q_norm_g[i], k_norm_g[i], tabs, False)
        ctx = (kc_all[i], vc_all[i])
        if kind == 0:
            oT = _attention(qT, k, vT, ctx, None, None)
        elif kind == 1:
            oT = _attention(qT, k, vT, ctx, sink, win_bias)
        else:
            oT = _attention(qT, k, vT, ctx, None, _neighborhood_bias(rel_pos_bias[i // 3], n_s // GRID_W))
        xs, h2, affT = _post_attn(oT, xs, mods[i], row_s, norm_ffn_g[i], w_o[i], w_router[i])
        acc_s = _moe(h2, affT, w_gate[i], w_up[i], w_down[i], 1)

    y_prompt = _residual(xp, acc_p, mods[DEPTH - 1], row_p)
    y_sample = _residual(xs, acc_s, mods[DEPTH - 1], row_s)
    return (y_prompt, y_sample, jnp.stack(new_k, axis=1), jnp.stack(new_v, axis=1))
```

```python
import functools

import jax
import jax.numpy as jnp
from jax import lax
from jax.experimental import pallas as pl
from jax.experimental.pallas import tpu as pltpu

F32 = jnp.float32
BF16 = jnp.bfloat16
I32 = jnp.int32

D_MODEL = 1024
DEPTH = 4
GRID_W = 64
N_HEADS = 16
N_KV_HEADS = 4
GROUP = N_HEADS // N_KV_HEADS
HEAD_DIM = 64
AXIS_PAIRS = HEAD_DIM // 4
ROPE_THETA = 10000.0
QKV_DIM = (N_HEADS + 2 * N_KV_HEADS) * HEAD_DIM
Q_DIM = N_HEADS * HEAD_DIM
KV_DIM = N_KV_HEADS * HEAD_DIM
WINDOW = 128
KH = 8
KW = 16
N_EXPERTS = 16
CAPACITY_FACTOR = 2
SCALE = HEAD_DIM ** -0.5
LOG2E = 1.4426950408889634
NEG_INF = -1e30
EPS = 1e-6

LANES = 128
TOK = 256
VT_ROWS = 64 + 16
AHEAD = 2
N_SBUF = 4
ROWS_PER_TOK = TOK // GRID_W
COND_ROWS = 8
FFN_TM = 512
UNROLL = 8
BISECT_STEPS = 64
VMEM_LIMIT = 56 * 1024 * 1024


def _cparams(sem):
    return pltpu.CompilerParams(dimension_semantics=sem, vmem_limit_bytes=VMEM_LIMIT)


def _split3(x):
    hi = x.astype(BF16)
    r = x - hi.astype(F32)
    mid = r.astype(BF16)
    lo = (r - mid.astype(F32)).astype(BF16)
    return hi, mid, lo


def _dot(a, b, dims=(((1,), (0,)), ((), ()))):
    return lax.dot_general(a, b, dims, preferred_element_type=F32)


_NT = (((1,), (1,)), ((), ()))
_TN = (((0,), (0,)), ((), ()))


def _dot_exact_rhs(onehot_bf16, x_f32, dims=(((1,), (0,)), ((), ()))):
    hi, mid, lo = _split3(x_f32)
    return _dot(onehot_bf16, hi, dims) + _dot(onehot_bf16, mid, dims) + _dot(onehot_bf16, lo, dims)


def _dot_f32(a, b, dims):
    a_hi, a_mid, _ = _split3(a)
    b_hi, b_mid, _ = _split3(b)
    return (_dot(a_hi, b_hi, dims) + _dot(a_hi, b_mid, dims) + _dot(a_mid, b_hi, dims))


def _mod_rmsnorm(x, g, shift, scale):
    ms = jnp.mean(x * x, axis=-1, keepdims=True)
    return (x * lax.rsqrt(ms + EPS) * g) * (1.0 + scale) + shift


ADA_TN = 1536


def _ada_kernel(cond_ref, w_ref, b_ref, out_ref):
    cnd = cond_ref[...]
    act = cnd * (1.0 / (1.0 + jnp.exp(-cnd)))
    out_ref[0] = _dot_f32(act, w_ref[0], (((1,), (0,)), ((), ()))) + b_ref[0]


def _ada_modulation(cond, w_ada, b_ada):
    n_out = 6 * D_MODEL
    return pl.pallas_call(
        _ada_kernel,
        grid=(DEPTH, n_out // ADA_TN),
        in_specs=[
            pl.BlockSpec((COND_ROWS, D_MODEL), lambda i, n: (0, 0)),
            pl.BlockSpec((1, D_MODEL, ADA_TN), lambda i, n: (i, 0, n)),
            pl.BlockSpec((1, 1, ADA_TN), lambda i, n: (i, 0, n)),
        ],
        out_specs=pl.BlockSpec((1, COND_ROWS, ADA_TN), lambda i, n: (i, 0, n)),
        out_shape=jax.ShapeDtypeStruct((DEPTH, COND_ROWS, n_out), F32),
        compiler_params=_cparams(("arbitrary", "arbitrary")),
        name="ada_modulation",
    )(cond, w_ada, b_ada.reshape(DEPTH, 1, n_out))


def _pre_attn_kernel(*refs, fuse_residual, rope, emit_cache):
    it = iter(refs)
    x_ref = next(it)
    if fuse_residual:
        acc_ref = next(it)
        pmod_ref = next(it)
    mod_ref = next(it)
    g_ref = next(it)
    w_ref = next(it)
    gq_ref = next(it)
    gk_ref = next(it)
    if rope:
        cos_ref = next(it)
        sin_ref = next(it)
    if fuse_residual:
        xo_ref = next(it)
    qT_ref = next(it)
    k_ref = next(it)
    vT_ref = next(it)
    if emit_cache:
        ck_ref = next(it)
        cv_ref = next(it)
    w_s = next(it)

    @pl.when((pl.program_id(0) == 0) & (pl.program_id(1) == 0))
    def _():
        w_s[...] = w_ref[0].astype(BF16)

    x = x_ref[0]
    if fuse_residual:
        x = x + pmod_ref[0, 5:6, :] * acc_ref[0]
        xo_ref[0] = x
    h = _mod_rmsnorm(x, g_ref[...], mod_ref[0, 0:1, :], mod_ref[0, 1:2, :])
    qkv = _dot(h.astype(BF16), w_s[...])
    if emit_cache:
        cv_ref[0] = qkv[:, Q_DIM + KV_DIM:]
    qkvT = qkv.T

    def head_norm(blk, gain):
        ss = jnp.sum(blk * blk, axis=0, keepdims=True)
        return blk * lax.rsqrt(ss * (1.0 / HEAD_DIM) + EPS) * gain

    def rotate(blk):
        if not rope:
            return blk
        cs = cos_ref[...]
        sn = sin_ref[...]
        parts = []
        for a in range(2):
            x1 = blk[32 * a:32 * a + 16]
            x2 = blk[32 * a + 16:32 * a + 32]
            c = cs[16 * a:16 * a + 16]
            s = sn[16 * a:16 * a + 16]
            parts.append(x1 * c - x2 * s)
            parts.append(x2 * c + x1 * s)
        return jnp.concatenate(parts, axis=0)

    gq = gq_ref[...]
    gk = gk_ref[...]
    for hd in range(N_HEADS):
        blk = qkvT[HEAD_DIM * hd:HEAD_DIM * (hd + 1)]
        qT_ref[0, hd] = (rotate(head_norm(blk, gq)) * (SCALE * LOG2E)).astype(BF16)
    kn_all = []
    for kv in range(N_KV_HEADS):
        off = Q_DIM + HEAD_DIM * kv
        kn_all.append(rotate(head_norm(qkvT[off:off + HEAD_DIM], gk)))
        offv = Q_DIM + KV_DIM + HEAD_DIM * kv
        vT_ref[0, kv, 0, :HEAD_DIM] = qkvT[offv:offv + HEAD_DIM].astype(BF16)
        vT_ref[0, kv, 0, HEAD_DIM:] = jnp.ones((VT_ROWS - HEAD_DIM, TOK), BF16)
    kn_tok = jnp.concatenate(kn_all, axis=0).T
    for kv in range(N_KV_HEADS):
        k_ref[0, kv, 0] = kn_tok[:, HEAD_DIM * kv:HEAD_DIM * (kv + 1)].astype(BF16)
    if emit_cache:
        ck_ref[0] = kn_tok


def _pre_attn(x, acc, prev_mod, mod, mod_row, g, w_qkv, layer, gq, gk, rope_tabs, emit_cache):
    b, n, _ = x.shape
    nt = n // TOK
    fuse = acc is not None
    rope = rope_tabs is not None
    tok_spec = pl.BlockSpec((1, TOK, D_MODEL), lambda bi, t: (bi, t, 0))
    mod_spec = pl.BlockSpec((1, 6, D_MODEL), lambda bi, t: (mod_row(bi), 0, 0))
    ins, in_specs = [x], [tok_spec]
    if fuse:
        ins += [acc, prev_mod]
        in_specs += [tok_spec, mod_spec]
    ins += [mod, g.reshape(1, D_MODEL), w_qkv,
            jnp.broadcast_to(gq[:, None], (HEAD_DIM, TOK)),
            jnp.broadcast_to(gk[:, None], (HEAD_DIM, TOK))]
    in_specs += [
        mod_spec,
        pl.BlockSpec((1, D_MODEL), lambda bi, t: (0, 0)),
        pl.BlockSpec((1, D_MODEL, QKV_DIM), lambda bi, t: (layer, 0, 0)),
        pl.BlockSpec((HEAD_DIM, TOK), lambda bi, t: (0, 0)),
        pl.BlockSpec((HEAD_DIM, TOK), lambda bi, t: (0, 0)),
    ]
    if rope:
        ins += list(rope_tabs)
        in_specs += [pl.BlockSpec((2 * AXIS_PAIRS, TOK), lambda bi, t: (0, t))] * 2
    out_shape, out_specs = [], []
    if fuse:
        out_shape.append(jax.ShapeDtypeStruct((b, n, D_MODEL), F32))
        out_specs.append(tok_spec)
    out_shape += [
        jax.ShapeDtypeStruct((b, N_HEADS, HEAD_DIM, n), BF16),
        jax.ShapeDtypeStruct((b, N_KV_HEADS, nt, TOK, HEAD_DIM), BF16),
        jax.ShapeDtypeStruct((b, N_KV_HEADS, nt, VT_ROWS, TOK), BF16),
    ]
    out_specs += [
        pl.BlockSpec((1, N_HEADS, HEAD_DIM, TOK), lambda bi, t: (bi, 0, 0, t)),
        pl.BlockSpec((1, N_KV_HEADS, 1, TOK, HEAD_DIM), lambda bi, t: (bi, 0, t, 0, 0)),
        pl.BlockSpec((1, N_KV_HEADS, 1, VT_ROWS, TOK), lambda bi, t: (bi, 0, t, 0, 0)),
    ]
    if emit_cache:
        out_shape += [jax.ShapeDtypeStruct((b, n, KV_DIM), F32)] * 2
        out_specs += [pl.BlockSpec((1, TOK, KV_DIM), lambda bi, t: (bi, t, 0))] * 2
    outs = pl.pallas_call(
        functools.partial(_pre_attn_kernel, fuse_residual=fuse, rope=rope, emit_cache=emit_cache),
        grid=(b, nt),
        in_specs=in_specs,
        out_specs=out_specs,
        out_shape=out_shape,
        scratch_shapes=[pltpu.VMEM((D_MODEL, QKV_DIM), BF16)],
        compiler_params=_cparams(("arbitrary", "arbitrary")),
        name="pre_attn",
    )(*ins)
    outs = list(outs)
    x_new = outs.pop(0) if fuse else x
    return [x_new] + outs


def _attn_kernel(*refs, mode, n_local, has_ctx, has_sink, bias_heads):
    it = iter(refs)
    qT_ref = next(it)
    if mode == "dense":
        k_ref = next(it)
        vT_ref = next(it)
    else:
        k_refs = [next(it) for _ in range(3)]
        vT_refs = [next(it) for _ in range(3)]
        bias_ref = next(it)
    if has_ctx:
        kc_ref = next(it)
        vcT_ref = next(it)
    if has_sink:
        sink_ref = next(it)
    oT_ref = next(it)
    m_s = next(it)
    acc_s = next(it)
    s_buf = next(it)

    kv = pl.program_id(1)
    for hd in range(GROUP):
        acc_s[hd, :HEAD_DIM] = jnp.zeros((HEAD_DIM, TOK), F32)
        if has_sink:
            m_s[hd] = jnp.full((1, TOK), sink_ref[kv * GROUP + hd] * LOG2E, F32)
            acc_s[hd, HEAD_DIM:] = jnp.ones((VT_ROWS - HEAD_DIM, TOK), F32)
        else:
            m_s[hd] = jnp.full((1, TOK), NEG_INF, F32)
            acc_s[hd, HEAD_DIM:] = jnp.zeros((VT_ROWS - HEAD_DIM, TOK), F32)

    def scores(par, kch):
        for hd in range(GROUP):
            s_buf[par, hd] = _dot(kch, qT_ref[0, hd])

    def consume(par, vch, bias_of_head):
        for hd in range(GROUP):
            s = s_buf[par, hd]
            if bias_of_head is not None:
                s = s + bias_of_head(hd)
            m_prev = m_s[hd]
            m_new = jnp.maximum(m_prev, jnp.max(s, axis=0, keepdims=True))
            alpha = jnp.exp2(m_prev - m_new)
            p = jnp.exp2(s - m_new)
            acc_s[hd] = alpha * acc_s[hd] + _dot(vch, p.astype(BF16))
            m_s[hd] = m_new

    static_chunks = []
    if mode == "band":
        for slot in range(3):
            if bias_heads == 1:
                bias_fn = lambda hd, slot=slot: bias_ref[0, 0, slot]
            else:
                bias_fn = lambda hd, slot=slot: bias_ref[0, hd, slot]
            static_chunks.append((k_refs[slot].at[0, 0, 0], vT_refs[slot].at[0, 0, 0], bias_fn))
    if has_ctx:
        for c in range(kc_ref.shape[2]):
            static_chunks.append((kc_ref.at[0, 0, c], vcT_ref.at[0, 0, c], None))

    n_loop = 0
    if mode == "dense":
        n_loop = max(0, (n_local - AHEAD) // N_SBUF) * N_SBUF
        static_chunks = ([(k_ref.at[0, 0, c], vT_ref.at[0, 0, c], None) for c in range(n_loop, n_local)]
                         + static_chunks)
    for c in range(AHEAD):
        if c < n_loop:
            scores(c % N_SBUF, k_ref[0, 0, c])
        elif c - n_loop < len(static_chunks):
            scores(c % N_SBUF, static_chunks[c - n_loop][0][...])
    if n_loop:
        def body(i, carry):
            c0 = N_SBUF * i
            for u in range(N_SBUF):
                scores((u + AHEAD) % N_SBUF, k_ref[0, 0, c0 + u + AHEAD])
                consume(u, vT_ref[0, 0, c0 + u], None)
            return carry
        lax.fori_loop(0, n_loop // N_SBUF, body, 0)
    for i, (_, v_at, bias_fn) in enumerate(static_chunks):
        if i + AHEAD < len(static_chunks):
            scores((n_loop + i + AHEAD) % N_SBUF, static_chunks[i + AHEAD][0][...])
        consume((n_loop + i) % N_SBUF, v_at[...], bias_fn)

    for hd in range(GROUP):
        oT_ref[0, hd] = (acc_s[hd, :HEAD_DIM] / acc_s[hd, HEAD_DIM:HEAD_DIM + 1]).astype(BF16)


def _attention(qT, k, vT, ctx, sink, bias):
    b, _, _, n = qT.shape
    nt = n // TOK
    mode = "dense" if bias is None else "band"
    ins = [qT]
    in_specs = [pl.BlockSpec((1, GROUP, HEAD_DIM, TOK), lambda bi, kv, j: (bi, kv, 0, j))]
    if mode == "dense":
        ins += [k, vT]
        in_specs += [
            pl.BlockSpec((1, 1, nt, TOK, HEAD_DIM), lambda bi, kv, j: (bi, kv, 0, 0, 0)),
            pl.BlockSpec((1, 1, nt, VT_ROWS, TOK), lambda bi, kv, j: (bi, kv, 0, 0, 0)),
        ]
        bias_heads = 0
    else:
        def base(j):
            return jnp.clip(j - 1, 0, nt - 3)
        ins += [k, k, k, vT, vT, vT, bias]
        for slot in range(3):
            in_specs.append(pl.BlockSpec(
                (1, 1, 1, TOK, HEAD_DIM),
                lambda bi, kv, j, slot=slot: (bi, kv, base(j) + slot, 0, 0)))
        for slot in range(3):
            in_specs.append(pl.BlockSpec(
                (1, 1, 1, VT_ROWS, TOK),
                lambda bi, kv, j, slot=slot: (bi, kv, base(j) + slot, 0, 0)))
        bias_heads = 1 if bias.shape[1] == 1 else GROUP

        def variant(j):
            return (j > 0).astype(I32) + (j == nt - 1).astype(I32)
        if bias_heads == 1:
            in_specs.append(pl.BlockSpec((1, 1, 3, TOK, TOK), lambda bi, kv, j: (variant(j), 0, 0, 0, 0)))
        else:
            in_specs.append(pl.BlockSpec((1, GROUP, 3, TOK, TOK), lambda bi, kv, j: (variant(j), kv, 0, 0, 0)))
    if ctx is not None:
        kc, vcT = ctx
        nc = kc.shape[2]
        ins += [kc, vcT]
        in_specs += [
            pl.BlockSpec((1, 1, nc, TOK, HEAD_DIM), lambda bi, kv, j: (bi, kv, 0, 0, 0)),
            pl.BlockSpec((1, 1, nc, VT_ROWS, TOK), lambda bi, kv, j: (bi, kv, 0, 0, 0)),
        ]
    if sink is not None:
        ins.append(sink)
        in_specs.append(pl.BlockSpec(memory_space=pltpu.SMEM))
    return pl.pallas_call(
        functools.partial(_attn_kernel, mode=mode, n_local=nt, has_ctx=ctx is not None,
                          has_sink=sink is not None, bias_heads=bias_heads),
        grid=(b, N_KV_HEADS, nt),
        in_specs=in_specs,
        out_specs=pl.BlockSpec((1, GROUP, HEAD_DIM, TOK), lambda bi, kv, j: (bi, kv, 0, j)),
        out_shape=jax.ShapeDtypeStruct((b, N_HEADS, HEAD_DIM, n), BF16),
        scratch_shapes=[
            pltpu.VMEM((GROUP, 1, TOK), F32),
            pltpu.VMEM((GROUP, VT_ROWS, TOK), F32),
            pltpu.VMEM((N_SBUF, GROUP, TOK, TOK), F32),
        ],
        compiler_params=_cparams(("arbitrary", "arbitrary", "arbitrary")),
        name="attention_" + mode,
    )(*ins)


def _post_attn_kernel(oT_ref, x_ref, mod_ref, g_ref, wo_ref, wrT_ref, xm_ref, h2_ref, affT_ref, wo_s):
    @pl.when((pl.program_id(0) == 0) & (pl.program_id(1) == 0))
    def _():
        wo_s[...] = wo_ref[0].astype(BF16)

    oT = oT_ref[0].reshape(Q_DIM, TOK)
    y = _dot(oT, wo_s[...], _TN)
    xm = x_ref[0] + mod_ref[0, 2:3, :] * y
    xm_ref[0] = xm
    h2 = _mod_rmsnorm(xm, g_ref[...], mod_ref[0, 3:4, :], mod_ref[0, 4:5, :])
    h2_ref[0] = h2
    logitsT = _dot_f32(wrT_ref[...], h2, _NT)
    mx = jnp.max(logitsT, axis=0, keepdims=True)
    e = jnp.exp(logitsT - mx)
    affT_ref[0] = e / jnp.sum(e, axis=0, keepdims=True)


def _post_attn(oT, x, mod, mod_row, g, w_o, layer, w_router):
    b, n, _ = x.shape
    nt = n // TOK
    tok_spec = pl.BlockSpec((1, TOK, D_MODEL), lambda bi, t: (bi, t, 0))
    return pl.pallas_call(
        _post_attn_kernel,
        grid=(b, nt),
        in_specs=[
            pl.BlockSpec((1, N_HEADS, HEAD_DIM, TOK), lambda bi, t: (bi, 0, 0, t)),
            tok_spec,
            pl.BlockSpec((1, 6, D_MODEL), lambda bi, t: (mod_row(bi), 0, 0)),
            pl.BlockSpec((1, D_MODEL), lambda bi, t: (0, 0)),
            pl.BlockSpec((1, Q_DIM, D_MODEL), lambda bi, t: (layer, 0, 0)),
            pl.BlockSpec((N_EXPERTS, D_MODEL), lambda bi, t: (0, 0)),
        ],
        out_specs=[tok_spec, tok_spec, pl.BlockSpec((1, N_EXPERTS, TOK), lambda bi, t: (bi, 0, t))],
        out_shape=[
            jax.ShapeDtypeStruct((b, n, D_MODEL), F32),
            jax.ShapeDtypeStruct((b, n, D_MODEL), F32),
            jax.ShapeDtypeStruct((b, N_EXPERTS, n), F32),
        ],
        scratch_shapes=[pltpu.VMEM((Q_DIM, D_MODEL), BF16)],
        compiler_params=_cparams(("arbitrary", "arbitrary")),
        name="post_attn",
    )(oT, x, mod, g.reshape(1, D_MODEL), w_o, w_router.T)


def _select_kernel(ag_ref, a_ref, idx_ref, gate_ref, key_s, cx_s, ci_s, *, groups, chunks, cap):
    rows = groups * chunks
    slots = max(cap, LANES)
    a = a_ref[0]

    ri = lax.broadcasted_iota(I32, (rows, rows), 0)
    ci = lax.broadcasted_iota(I32, (rows, rows), 1)
    same = (ri // chunks) == (ci // chunks)
    blockdiag = jnp.where(same, 1.0, 0.0).astype(BF16)
    lower = jnp.where(same & (ci < ri), 1.0, 0.0).astype(BF16)
    li = lax.broadcasted_iota(I32, (LANES, LANES), 0)
    lj = lax.broadcasted_iota(I32, (LANES, LANES), 1)
    upper_incl = jnp.where(li <= lj, 1.0, 0.0).astype(BF16)
    ones_ll = jnp.ones((LANES, LANES), BF16)

    def as_bf16(mask):
        return jnp.where(mask, 1.0, 0.0).astype(BF16)

    def row_total(x01):
        return _dot(x01, ones_ll)

    def expert_total(x01):
        return _dot(blockdiag, row_total(x01).astype(BF16))

    capf = float(cap)

    ag = ag_ref[0]

    def search(step, bounds):
        lo, hi = bounds
        mid = 0.5 * (lo + hi)
        ok = jnp.sum(jnp.where(ag >= mid, 1.0, 0.0), axis=1, keepdims=True) >= capf
        return jnp.where(ok, mid, lo), jnp.where(ok, hi, mid)

    lo_g, hi_g = lax.fori_loop(0, BISECT_STEPS, search,
                               (jnp.zeros((groups, 1), F32), jnp.full((groups, 1), 2.0, F32)))
    to_rows = as_bf16(lax.broadcasted_iota(I32, (rows, groups), 0) // chunks
                      == lax.broadcasted_iota(I32, (rows, groups), 1))
    lo = _dot_exact_rhs(to_rows, jnp.broadcast_to(lo_g, (groups, LANES)))
    hi = _dot_exact_rhs(to_rows, jnp.broadcast_to(hi_g, (groups, LANES)))

    gt = a >= hi
    eq = (a >= lo) & (a < hi)
    need = capf - expert_total(as_bf16(gt))
    eq01 = as_bf16(eq)
    eq_rank = (_dot(eq01, upper_incl) - eq01.astype(F32)
               + _dot(lower, row_total(eq01).astype(BF16)))
    sel = gt | (eq & (eq_rank < need))
    sel01 = as_bf16(sel)
    sel_rows = row_total(sel01)
    c_excl = _dot(lower, sel_rows.astype(BF16))
    c_incl = c_excl + sel_rows
    key = 2.0 * (_dot(sel01, upper_incl) + c_excl) - sel01.astype(F32)
    first_lane = as_bf16(lax.broadcasted_iota(I32, (8, LANES), 1) == 0)
    cx_s[...] = _dot_exact_rhs(first_lane, c_excl, _NT)
    ci_s[...] = _dot_exact_rhs(first_lane, c_incl, _NT)
    key_s[...] = key

    rb = min(rows, LANES)
    jcol = lax.broadcasted_iota(I32, (slots, rb), 0).astype(F32)
    rlane = lax.broadcasted_iota(I32, (slots, rb), 1)
    jl = lax.broadcasted_iota(I32, (slots, LANES), 0).astype(F32)
    lane8 = lax.broadcasted_iota(I32, (8, LANES), 1).astype(BF16)
    chunk8 = (lax.broadcasted_iota(I32, (8, rb), 1) % chunks).astype(BF16)
    ones8 = jnp.ones((8, LANES), BF16)

    def per_group(e, carry):
        r0 = pl.multiple_of(((e * chunks) // rb) * rb, rb)
        mine = ((rlane + r0) // chunks) == e
        c_excl_l = cx_s[0:1, pl.ds(r0, rb)]
        c_incl_l = ci_s[0:1, pl.ds(r0, rb)]
        onehot = as_bf16(mine & (c_excl_l <= jcol) & (jcol < c_incl_l))
        hit = _dot_exact_rhs(onehot, key_s[pl.ds(r0, rb), :]) == 2.0 * jl + 1.0
        hit01 = as_bf16(hit)
        tok = _dot(chunk8, onehot, _NT) * float(LANES) + _dot(lane8, hit01, _NT)
        gsel = jnp.where(hit, _dot_exact_rhs(onehot, a_ref[0, pl.ds(r0, rb), :]), 0.0)
        g = _dot_exact_rhs(ones8, gsel, _NT)
        idx_ref[0, pl.ds(e, 1), :] = tok[0:1, :cap].astype(I32)
        gate_ref[0, pl.ds(e, 1), :] = g[0:1, :cap]
        return carry

    lax.fori_loop(0, groups, per_group, 0, unroll=4 if slots == LANES else 2)


def _select(affT, cap):
    b, _, n = affT.shape
    chunks = n // LANES
    bstep = b if b * N_EXPERTS * chunks <= 512 else 1
    groups = bstep * N_EXPERTS
    rows = groups * chunks
    idx, gate = pl.pallas_call(
        functools.partial(_select_kernel, groups=groups, chunks=chunks, cap=cap),
        grid=(b // bstep,),
        in_specs=[pl.BlockSpec((1, groups, n), lambda bi: (bi, 0, 0)),
                  pl.BlockSpec((1, rows, LANES), lambda bi: (bi, 0, 0))],
        out_specs=[pl.BlockSpec((1, groups, cap), lambda bi: (bi, 0, 0))] * 2,
        out_shape=[jax.ShapeDtypeStruct((b // bstep, groups, cap), I32),
                   jax.ShapeDtypeStruct((b // bstep, groups, cap), F32)],
        scratch_shapes=[pltpu.VMEM((rows, LANES), F32), pltpu.VMEM((8, rows), F32), pltpu.VMEM((8, rows), F32)],
        compiler_params=_cparams(("arbitrary",)),
        name="select",
    )(affT.reshape(b // bstep, groups, n), affT.reshape(b // bstep, rows, LANES))
    return idx.reshape(b, N_EXPERTS, cap), gate.reshape(b, N_EXPERTS, cap)


def _dispatch_kernel(idx_ref, h_ref, xg_ref, buf, *, cap, eg):
    bi = pl.program_id(0)
    g0 = pl.program_id(1) * eg

    def per_expert(el, carry):
        base = ((bi * N_EXPERTS) + g0 + el) * cap

        def body(jo, c2):
            j0 = pl.multiple_of(jo * UNROLL, UNROLL)
            dst = buf.at[pl.ds(j0, UNROLL)]
            for u in range(UNROLL):
                t = idx_ref[base + j0 + u]
                dst[u:u + 1, :] = h_ref[0, pl.ds(t, 1), :]
            return c2

        lax.fori_loop(0, cap // UNROLL, body, 0)
        xg_ref[el, 0] = buf[...].astype(BF16)
        return carry

    lax.fori_loop(0, eg, per_expert, 0)


def _dispatch(idx, h2, cap, eg):
    b, n, _ = h2.shape
    grid_spec = pltpu.PrefetchScalarGridSpec(
        num_scalar_prefetch=1,
        grid=(b, N_EXPERTS // eg),
        in_specs=[pl.BlockSpec((1, n, D_MODEL), lambda bi, g, idx_r: (bi, 0, 0))],
        out_specs=pl.BlockSpec((eg, 1, cap, D_MODEL), lambda bi, g, idx_r: (g, bi, 0, 0)),
        scratch_shapes=[pltpu.VMEM((cap, D_MODEL), F32)],
    )
    return pl.pallas_call(
        functools.partial(_dispatch_kernel, cap=cap, eg=eg),
        grid_spec=grid_spec,
        out_shape=jax.ShapeDtypeStruct((N_EXPERTS, b, cap, D_MODEL), BF16),
        compiler_params=_cparams(("arbitrary", "arbitrary")),
        name="dispatch",
    )(idx.reshape(-1), h2)


def _combine_kernel(idx_ref, y_ref, acc_ref, *, cap, eg):
    bi = pl.program_id(0)
    g = pl.program_id(1)
    g0 = g * eg

    @pl.when(g == 0)
    def _():
        acc_ref[...] = jnp.zeros_like(acc_ref)

    def per_expert(el, carry):
        base = ((bi * N_EXPERTS) + g0 + el) * cap

        def body(jo, c2):
            j0 = pl.multiple_of(jo * UNROLL, UNROLL)
            toks = [idx_ref[base + j0 + u] for u in range(UNROLL)]
            olds = [acc_ref[0, pl.ds(t, 1), :] for t in toks]
            src = y_ref.at[el, 0, pl.ds(j0, UNROLL)]
            for u in range(UNROLL):
                acc_ref[0, pl.ds(toks[u], 1), :] = olds[u] + src[u:u + 1, :]
            return c2

        lax.fori_loop(0, cap // UNROLL, body, 0)
        return carry

    lax.fori_loop(0, eg, per_expert, 0)


def _combine(idx, y, n, cap, eg):
    b = y.shape[1]
    grid_spec = pltpu.PrefetchScalarGridSpec(
        num_scalar_prefetch=1,
        grid=(b, N_EXPERTS // eg),
        in_specs=[pl.BlockSpec((eg, 1, cap, D_MODEL), lambda bi, g, idx_r: (g, bi, 0, 0))],
        out_specs=pl.BlockSpec((1, n, D_MODEL), lambda bi, g, idx_r: (bi, 0, 0)),
    )
    return pl.pallas_call(
        functools.partial(_combine_kernel, cap=cap, eg=eg),
        grid_spec=grid_spec,
        out_shape=jax.ShapeDtypeStruct((b, n, D_MODEL), F32),
        compiler_params=_cparams(("arbitrary", "arbitrary")),
        name="combine",
    )(idx.reshape(-1), y)


def _ffn_kernel(xg_ref, gate_ref, wg_ref, wu_ref, wd_ref, y_ref, wg_s, wu_s, wd_s):
    @pl.when(pl.program_id(1) == 0)
    def _():
        wg_s[...] = wg_ref[0, 0].astype(BF16)
        wu_s[...] = wu_ref[0, 0].astype(BF16)
        wd_s[...] = wd_ref[0, 0].astype(BF16)

    x = xg_ref[0]
    a = _dot(x, wg_s[...])
    u = _dot(x, wu_s[...])
    hmid = (a * (1.0 / (1.0 + jnp.exp(-a))) * u).astype(BF16)
    y_ref[0] = _dot(hmid, wd_s[...]) * gate_ref[0]


def _ffn(xg, gate, w_gate, w_up, w_down, layer):
    _, r, _ = xg.shape
    tm = min(FFN_TM, r)
    w_spec = pl.BlockSpec((1, 1, D_MODEL, D_MODEL), lambda e, m: (layer, e, 0, 0))
    return pl.pallas_call(
        _ffn_kernel,
        grid=(N_EXPERTS, r // tm),
        in_specs=[
            pl.BlockSpec((1, tm, D_MODEL), lambda e, m: (e, m, 0)),
            pl.BlockSpec((1, tm, 1), lambda e, m: (e, m, 0)),
            w_spec, w_spec, w_spec,
        ],
        out_specs=pl.BlockSpec((1, tm, D_MODEL), lambda e, m: (e, m, 0)),
        out_shape=jax.ShapeDtypeStruct((N_EXPERTS, r, D_MODEL), F32),
        scratch_shapes=[pltpu.VMEM((D_MODEL, D_MODEL), BF16)] * 3,
        compiler_params=_cparams(("arbitrary", "arbitrary")),
        name="expert_ffn",
    )(xg, gate, w_gate, w_up, w_down)


def _moe(h2, affT, w_gate, w_up, w_down, layer, eg):
    b, n, _ = h2.shape
    cap = (CAPACITY_FACTOR * n) // N_EXPERTS
    idx, gate = _select(affT, cap)
    xg = _dispatch(idx, h2, cap, eg)
    gate_rows = jnp.swapaxes(gate, 0, 1).reshape(N_EXPERTS, b * cap, 1)
    y = _ffn(xg.reshape(N_EXPERTS, b * cap, D_MODEL), gate_rows, w_gate, w_up, w_down, layer)
    return _combine(idx, y.reshape(N_EXPERTS, b, cap, D_MODEL), n, cap, eg)


def _residual_kernel(x_ref, acc_ref, mod_ref, out_ref):
    out_ref[0] = x_ref[0] + mod_ref[0, 5:6, :] * acc_ref[0]


def _residual(x, acc, mod, mod_row):
    b, n, _ = x.shape
    tok_spec = pl.BlockSpec((1, TOK, D_MODEL), lambda bi, t: (bi, t, 0))
    return pl.pallas_call(
        _residual_kernel,
        grid=(b, n // TOK),
        in_specs=[tok_spec, tok_spec, pl.BlockSpec((1, 6, D_MODEL), lambda bi, t: (mod_row(bi), 0, 0))],
        out_specs=tok_spec,
        out_shape=jax.ShapeDtypeStruct((b, n, D_MODEL), F32),
        compiler_params=_cparams(("arbitrary", "arbitrary")),
        name="ffn_residual",
    )(x, acc, mod)


def _rope_tables(n):
    t = jnp.arange(n)
    inv_freq = ROPE_THETA ** (-jnp.arange(AXIS_PAIRS, dtype=F32) / AXIS_PAIRS)
    ang_r = (t // GRID_W).astype(F32)[None, :] * inv_freq[:, None]
    ang_c = (t % GRID_W).astype(F32)[None, :] * inv_freq[:, None]
    ang = jnp.concatenate([ang_r, ang_c], axis=0)
    return jnp.cos(ang), jnp.sin(ang)


def _band_chunk_starts(nt):
    return (0, None, nt - 3)


def _window_bias(nt):
    u = jnp.arange(TOK)[:, None]
    q = jnp.arange(TOK)[None, :]
    variants = []
    for rel0 in (0, -1, -2):
        slots = []
        for slot in range(3):
            d = (rel0 + slot) * TOK + u - q
            slots.append(jnp.where(jnp.abs(d) <= WINDOW, 0.0, NEG_INF).astype(F32))
        variants.append(jnp.stack(slots))
    return jnp.stack(variants)[:, None]


def _neighborhood_bias(rpb, rows):
    nt = rows // ROWS_PER_TOK
    kc = jnp.arange(GRID_W)[:, None]
    qc = jnp.arange(GRID_W)[None, :]
    cs = jnp.clip(qc - KW // 2, 0, GRID_W - KW)
    col_ok = (kc >= cs) & (kc < cs + KW)
    diag = (kc - qc + (KW - 1))[None] == jnp.arange(2 * KW - 1)[:, None, None]
    toep = jnp.einsum("hdj,jkq->hdkq", rpb.astype(F32) * LOG2E, diag.astype(F32),
                      precision=lax.Precision.HIGHEST)
    toep = jnp.where(col_ok, toep, NEG_INF)
    masked = jnp.full((rpb.shape[0], GRID_W, GRID_W), NEG_INF, F32)
    variants = []
    for j, chunk0 in ((0, 0), (1, 0), (nt - 1, nt - 3)):
        slots = []
        for slot in range(3):
            block_rows = []
            for a in range(ROWS_PER_TOK):
                kr = (chunk0 + slot) * ROWS_PER_TOK + a
                blocks = []
                for b in range(ROWS_PER_TOK):
                    qr = j * ROWS_PER_TOK + b
                    rs = min(max(qr - KH // 2, 0), rows - KH)
                    ok = rs <= kr < rs + KH
                    blocks.append(toep[:, kr - qr + (KH - 1)] if ok else masked)
                block_rows.append(jnp.concatenate(blocks, axis=-1))
            slots.append(jnp.concatenate(block_rows, axis=-2))
        variants.append(jnp.stack(slots, axis=1))
    return jnp.stack(variants)


def kernel(x_prompt, x_sample, cache_k, cache_v, c, c_ctx, w_ada, b_ada, norm_attn_g, norm_ffn_g,
           w_qkv, q_norm_g, k_norm_g, w_o, sink_logits, rel_pos_bias, w_router, w_gate, w_up, w_down):
    bp, n_p, _ = x_prompt.shape
    bs, n_s, _ = x_sample.shape
    past = cache_k.shape[2]
    assert n_p % TOK == 0 and n_s % TOK == 0 and past % TOK == 0
    assert n_s // TOK >= 3 and 1 + bs <= COND_ROWS

    cond = jnp.zeros((COND_ROWS, D_MODEL), F32).at[0].set(c_ctx).at[1:1 + bs].set(c)
    mods = _ada_modulation(cond, w_ada, b_ada).reshape(DEPTH, COND_ROWS, 6, D_MODEL)
    row_p = lambda bi: 0
    row_s = lambda bi: bi + 1

    nc = past // TOK
    kc_all = jnp.transpose(cache_k, (1, 0, 3, 2, 4)).astype(BF16)
    kc_all = kc_all.reshape(DEPTH, bs, N_KV_HEADS, nc, TOK, HEAD_DIM)
    vc_all = jnp.transpose(cache_v, (1, 0, 3, 4, 2)).astype(BF16)
    vc_all = jnp.transpose(vc_all.reshape(DEPTH, bs, N_KV_HEADS, HEAD_DIM, nc, TOK), (0, 1, 2, 4, 3, 5))
    vc_all = jnp.concatenate(
        [vc_all, jnp.ones((DEPTH, bs, N_KV_HEADS, nc, VT_ROWS - HEAD_DIM, TOK), BF16)], axis=4)

    rope_tabs = _rope_tables(n_s)
    win_bias = _window_bias(n_s // TOK)

    xp, xs = x_prompt, x_sample
    acc_p = acc_s = None
    new_k, new_v = [], []
    for i in range(DEPTH):
        kind = i % 3
        sink = sink_logits[i // 3] if kind == 1 else None
        prev = mods[i - 1] if i > 0 else None

        xp, qT, k, vT, ck, cv = _pre_attn(xp, acc_p, prev, mods[i], row_p, norm_attn_g[i], w_qkv, i,
                                          q_norm_g[i], k_norm_g[i], None, True)
        new_k.append(ck.reshape(bp, n_p, N_KV_HEADS, HEAD_DIM))
        new_v.append(cv.reshape(bp, n_p, N_KV_HEADS, HEAD_DIM))
        oT = _attention(qT, k, vT, None, sink, None)
        xp, h2, affT = _post_attn(oT, xp, mods[i], row_p, norm_ffn_g[i], w_o, i, w_router[i])
        acc_p = _moe(h2, affT, w_gate, w_up, w_down, i, N_EXPERTS)

        tabs = rope_tabs if kind != 2 else None
        xs, qT, k, vT = _pre_attn(xs, acc_s, prev, mods[i], row_s, norm_attn_g[i], w_qkv, i,
                                  q_norm_g[i], k_norm_g[i], tabs, False)
        ctx = (kc_all[i], vc_all[i])
        if kind == 0:
            oT = _attention(qT, k, vT, ctx, None, None)
        elif kind == 1:
            oT = _attention(qT, k, vT, ctx, sink, win_bias)
        else:
            oT = _attention(qT, k, vT, ctx, None, _neighborhood_bias(rel_pos_bias[i // 3], n_s // GRID_W))
        xs, h2, affT = _post_attn(oT, xs, mods[i], row_s, norm_ffn_g[i], w_o, i, w_router[i])
        acc_s = _moe(h2, affT, w_gate, w_up, w_down, i, 1)

    y_prompt = _residual(xp, acc_p, mods[DEPTH - 1], row_p)
    y_sample = _residual(xs, acc_s, mods[DEPTH - 1], row_s)
    return (y_prompt, y_sample, jnp.stack(new_k, axis=1), jnp.stack(new_v, axis=1))
```

```python
import functools

import jax
import jax.numpy as jnp
from jax import lax
from jax.experimental import pallas as pl
from jax.experimental.pallas import tpu as pltpu

F32 = jnp.float32
BF16 = jnp.bfloat16
I32 = jnp.int32

D_MODEL = 1024
DEPTH = 4
GRID_W = 64
N_HEADS = 16
N_KV_HEADS = 4
GROUP = N_HEADS // N_KV_HEADS
HEAD_DIM = 64
AXIS_PAIRS = HEAD_DIM // 4
ROPE_THETA = 10000.0
QKV_DIM = (N_HEADS + 2 * N_KV_HEADS) * HEAD_DIM
Q_DIM = N_HEADS * HEAD_DIM
KV_DIM = N_KV_HEADS * HEAD_DIM
WINDOW = 128
KH = 8
KW = 16
N_EXPERTS = 16
CAPACITY_FACTOR = 2
SCALE = HEAD_DIM ** -0.5
LOG2E = 1.4426950408889634
NEG_INF = -1e30
EPS = 1e-6

LANES = 128
TOK = 256
VT_ROWS = 64 + 16
AHEAD = 2
N_SBUF = 4
SUBTILES = 4
ROWS_PER_TOK = TOK // GRID_W
COND_ROWS = 8
FFN_TM = 512
FFN_SUB = 256
UNROLL = 8
BISECT_STEPS = 64
VMEM_LIMIT = 56 * 1024 * 1024


def _cparams(sem):
    return pltpu.CompilerParams(dimension_semantics=sem, vmem_limit_bytes=VMEM_LIMIT)


def _split3(x):
    hi = x.astype(BF16)
    r = x - hi.astype(F32)
    mid = r.astype(BF16)
    lo = (r - mid.astype(F32)).astype(BF16)
    return hi, mid, lo


def _dot(a, b, dims=(((1,), (0,)), ((), ()))):
    return lax.dot_general(a, b, dims, preferred_element_type=F32)


_NT = (((1,), (1,)), ((), ()))
_TN = (((0,), (0,)), ((), ()))


def _dot_exact_rhs(onehot_bf16, x_f32, dims=(((1,), (0,)), ((), ()))):
    hi, mid, lo = _split3(x_f32)
    return _dot(onehot_bf16, hi, dims) + _dot(onehot_bf16, mid, dims) + _dot(onehot_bf16, lo, dims)


def _dot_f32(a, b, dims):
    a_hi, a_mid, _ = _split3(a)
    b_hi, b_mid, _ = _split3(b)
    return (_dot(a_hi, b_hi, dims) + _dot(a_hi, b_mid, dims) + _dot(a_mid, b_hi, dims))


def _mod_rmsnorm(x, g, shift, scale):
    ms = jnp.mean(x * x, axis=-1, keepdims=True)
    return (x * lax.rsqrt(ms + EPS) * g) * (1.0 + scale) + shift


ADA_TN = 1536


def _ada_kernel(cond_ref, w_ref, b_ref, out_ref):
    cnd = cond_ref[...]
    act = cnd * (1.0 / (1.0 + jnp.exp(-cnd)))
    out_ref[0] = _dot_f32(act, w_ref[0], (((1,), (0,)), ((), ()))) + b_ref[0]


def _ada_modulation(cond, w_ada, b_ada):
    n_out = 6 * D_MODEL
    return pl.pallas_call(
        _ada_kernel,
        grid=(DEPTH, n_out // ADA_TN),
        in_specs=[
            pl.BlockSpec((COND_ROWS, D_MODEL), lambda i, n: (0, 0)),
            pl.BlockSpec((1, D_MODEL, ADA_TN), lambda i, n: (i, 0, n)),
            pl.BlockSpec((1, 1, ADA_TN), lambda i, n: (i, 0, n)),
        ],
        out_specs=pl.BlockSpec((1, COND_ROWS, ADA_TN), lambda i, n: (i, 0, n)),
        out_shape=jax.ShapeDtypeStruct((DEPTH, COND_ROWS, n_out), F32),
        compiler_params=_cparams(("arbitrary", "arbitrary")),
        name="ada_modulation",
    )(cond, w_ada, b_ada.reshape(DEPTH, 1, n_out))


def _pre_attn_kernel(*refs, fuse_residual, rope, emit_cache, subtiles):
    it = iter(refs)
    x_ref = next(it)
    if fuse_residual:
        acc_ref = next(it)
        pmod_ref = next(it)
    mod_ref = next(it)
    g_ref = next(it)
    w_ref = next(it)
    gq_ref = next(it)
    gk_ref = next(it)
    if rope:
        cos_ref = next(it)
        sin_ref = next(it)
    if fuse_residual:
        xo_ref = next(it)
    qT_ref = next(it)
    k_ref = next(it)
    vT_ref = next(it)
    if emit_cache:
        ck_ref = next(it)
        cv_ref = next(it)
    w_s = next(it)

    @pl.when((pl.program_id(0) == 0) & (pl.program_id(1) == 0))
    def _():
        w_s[...] = w_ref[0].astype(BF16)

    def head_norm(blk, gain):
        ss = jnp.sum(blk * blk, axis=0, keepdims=True)
        return blk * lax.rsqrt(ss * (1.0 / HEAD_DIM) + EPS) * gain

    def rotate(blk, tsl):
        if not rope:
            return blk
        cs = cos_ref[:, tsl]
        sn = sin_ref[:, tsl]
        parts = []
        for a in range(2):
            x1 = blk[32 * a:32 * a + 16]
            x2 = blk[32 * a + 16:32 * a + 32]
            c = cs[16 * a:16 * a + 16]
            s = sn[16 * a:16 * a + 16]
            parts.append(x1 * c - x2 * s)
            parts.append(x2 * c + x1 * s)
        return jnp.concatenate(parts, axis=0)

    gq = gq_ref[...]
    gk = gk_ref[...]
    for st in range(subtiles):
        tsl = slice(st * TOK, (st + 1) * TOK)
        x = x_ref[0, tsl, :]
        if fuse_residual:
            x = x + pmod_ref[0, 5:6, :] * acc_ref[0, tsl, :]
            xo_ref[0, tsl, :] = x
        h = _mod_rmsnorm(x, g_ref[...], mod_ref[0, 0:1, :], mod_ref[0, 1:2, :])
        qkv = _dot(h.astype(BF16), w_s[...])
        if emit_cache:
            cv_ref[0, tsl, :] = qkv[:, Q_DIM + KV_DIM:]
        qkvT = qkv.T
        for hd in range(N_HEADS):
            blk = qkvT[HEAD_DIM * hd:HEAD_DIM * (hd + 1)]
            qT_ref[0, hd, :, tsl] = (rotate(head_norm(blk, gq), tsl) * (SCALE * LOG2E)).astype(BF16)
        kn_all = []
        for kv in range(N_KV_HEADS):
            off = Q_DIM + HEAD_DIM * kv
            kn_all.append(rotate(head_norm(qkvT[off:off + HEAD_DIM], gk), tsl))
            offv = Q_DIM + KV_DIM + HEAD_DIM * kv
            vT_ref[0, kv, st, :HEAD_DIM] = qkvT[offv:offv + HEAD_DIM].astype(BF16)
            vT_ref[0, kv, st, HEAD_DIM:] = jnp.ones((VT_ROWS - HEAD_DIM, TOK), BF16)
        kn_tok = jnp.concatenate(kn_all, axis=0).T
        for kv in range(N_KV_HEADS):
            k_ref[0, kv, st] = kn_tok[:, HEAD_DIM * kv:HEAD_DIM * (kv + 1)].astype(BF16)
        if emit_cache:
            ck_ref[0, tsl, :] = kn_tok


def _pre_attn(x, acc, prev_mod, mod, mod_row, g, w_qkv, layer, gq, gk, rope_tabs, emit_cache):
    b, n, _ = x.shape
    nt = n // TOK
    sub = SUBTILES if nt % SUBTILES == 0 else 1
    fuse = acc is not None
    rope = rope_tabs is not None
    tok_spec = pl.BlockSpec((1, sub * TOK, D_MODEL), lambda bi, t: (bi, t, 0))
    mod_spec = pl.BlockSpec((1, 6, D_MODEL), lambda bi, t: (mod_row(bi), 0, 0))
    ins, in_specs = [x], [tok_spec]
    if fuse:
        ins += [acc, prev_mod]
        in_specs += [tok_spec, mod_spec]
    ins += [mod, g.reshape(1, D_MODEL), w_qkv,
            jnp.broadcast_to(gq[:, None], (HEAD_DIM, TOK)),
            jnp.broadcast_to(gk[:, None], (HEAD_DIM, TOK))]
    in_specs += [
        mod_spec,
        pl.BlockSpec((1, D_MODEL), lambda bi, t: (0, 0)),
        pl.BlockSpec((1, D_MODEL, QKV_DIM), lambda bi, t: (layer, 0, 0)),
        pl.BlockSpec((HEAD_DIM, TOK), lambda bi, t: (0, 0)),
        pl.BlockSpec((HEAD_DIM, TOK), lambda bi, t: (0, 0)),
    ]
    if rope:
        ins += list(rope_tabs)
        in_specs += [pl.BlockSpec((2 * AXIS_PAIRS, sub * TOK), lambda bi, t: (0, t))] * 2
    out_shape, out_specs = [], []
    if fuse:
        out_shape.append(jax.ShapeDtypeStruct((b, n, D_MODEL), F32))
        out_specs.append(tok_spec)
    out_shape += [
        jax.ShapeDtypeStruct((b, N_HEADS, HEAD_DIM, n), BF16),
        jax.ShapeDtypeStruct((b, N_KV_HEADS, nt, TOK, HEAD_DIM), BF16),
        jax.ShapeDtypeStruct((b, N_KV_HEADS, nt, VT_ROWS, TOK), BF16),
    ]
    out_specs += [
        pl.BlockSpec((1, N_HEADS, HEAD_DIM, sub * TOK), lambda bi, t: (bi, 0, 0, t)),
        pl.BlockSpec((1, N_KV_HEADS, sub, TOK, HEAD_DIM), lambda bi, t: (bi, 0, t, 0, 0)),
        pl.BlockSpec((1, N_KV_HEADS, sub, VT_ROWS, TOK), lambda bi, t: (bi, 0, t, 0, 0)),
    ]
    if emit_cache:
        out_shape += [jax.ShapeDtypeStruct((b, n, KV_DIM), F32)] * 2
        out_specs += [pl.BlockSpec((1, sub * TOK, KV_DIM), lambda bi, t: (bi, t, 0))] * 2
    outs = pl.pallas_call(
        functools.partial(_pre_attn_kernel, fuse_residual=fuse, rope=rope, emit_cache=emit_cache,
                          subtiles=sub),
        grid=(b, nt // sub),
        in_specs=in_specs,
        out_specs=out_specs,
        out_shape=out_shape,
        scratch_shapes=[pltpu.VMEM((D_MODEL, QKV_DIM), BF16)],
        compiler_params=_cparams(("arbitrary", "arbitrary")),
        name="pre_attn",
    )(*ins)
    outs = list(outs)
    x_new = outs.pop(0) if fuse else x
    return [x_new] + outs


def _attn_kernel(*refs, mode, n_local, has_ctx, has_sink, bias_heads):
    it = iter(refs)
    qT_ref = next(it)
    if mode == "dense":
        k_ref = next(it)
        vT_ref = next(it)
    else:
        k_refs = [next(it) for _ in range(3)]
        vT_refs = [next(it) for _ in range(3)]
        bias_ref = next(it)
    if has_ctx:
        kc_ref = next(it)
        vcT_ref = next(it)
    if has_sink:
        sink_ref = next(it)
    oT_ref = next(it)
    m_s = next(it)
    acc_s = next(it)
    s_buf = next(it)

    kv = pl.program_id(1)
    for hd in range(GROUP):
        acc_s[hd, :HEAD_DIM] = jnp.zeros((HEAD_DIM, TOK), F32)
        if has_sink:
            m_s[hd] = jnp.full((1, TOK), sink_ref[kv * GROUP + hd] * LOG2E, F32)
            acc_s[hd, HEAD_DIM:] = jnp.ones((VT_ROWS - HEAD_DIM, TOK), F32)
        else:
            m_s[hd] = jnp.full((1, TOK), NEG_INF, F32)
            acc_s[hd, HEAD_DIM:] = jnp.zeros((VT_ROWS - HEAD_DIM, TOK), F32)

    def scores(par, kch):
        for hd in range(GROUP):
            s_buf[par, hd] = _dot(kch, qT_ref[0, hd])

    def consume(par, vch, bias_of_head):
        for hd in range(GROUP):
            s = s_buf[par, hd]
            if bias_of_head is not None:
                s = s + bias_of_head(hd)
            m_prev = m_s[hd]
            m_new = jnp.maximum(m_prev, jnp.max(s, axis=0, keepdims=True))
            alpha = jnp.exp2(m_prev - m_new)
            p = jnp.exp2(s - m_new)
            acc_s[hd] = alpha * acc_s[hd] + _dot(vch, p.astype(BF16))
            m_s[hd] = m_new

    static_chunks = []
    if mode == "band":
        for slot in range(3):
            if bias_heads == 1:
                bias_fn = lambda hd, slot=slot: bias_ref[0, 0, slot]
            else:
                bias_fn = lambda hd, slot=slot: bias_ref[0, hd, slot]
            static_chunks.append((k_refs[slot].at[0, 0, 0], vT_refs[slot].at[0, 0, 0], bias_fn))
    if has_ctx:
        for c in range(kc_ref.shape[2]):
            static_chunks.append((kc_ref.at[0, 0, c], vcT_ref.at[0, 0, c], None))

    n_loop = 0
    if mode == "dense":
        n_loop = max(0, (n_local - AHEAD) // N_SBUF) * N_SBUF
        static_chunks = ([(k_ref.at[0, 0, c], vT_ref.at[0, 0, c], None) for c in range(n_loop, n_local)]
                         + static_chunks)
    for c in range(AHEAD):
        if c < n_loop:
            scores(c % N_SBUF, k_ref[0, 0, c])
        elif c - n_loop < len(static_chunks):
            scores(c % N_SBUF, static_chunks[c - n_loop][0][...])
    if n_loop:
        def body(i, carry):
            c0 = N_SBUF * i
            for u in range(N_SBUF):
                scores((u + AHEAD) % N_SBUF, k_ref[0, 0, c0 + u + AHEAD])
                consume(u, vT_ref[0, 0, c0 + u], None)
            return carry
        lax.fori_loop(0, n_loop // N_SBUF, body, 0)
    for i, (_, v_at, bias_fn) in enumerate(static_chunks):
        if i + AHEAD < len(static_chunks):
            scores((n_loop + i + AHEAD) % N_SBUF, static_chunks[i + AHEAD][0][...])
        consume((n_loop + i) % N_SBUF, v_at[...], bias_fn)

    for hd in range(GROUP):
        oT_ref[0, hd] = (acc_s[hd, :HEAD_DIM] / acc_s[hd, HEAD_DIM:HEAD_DIM + 1]).astype(BF16)


def _attention(qT, k, vT, ctx, sink, bias):
    b, _, _, n = qT.shape
    nt = n // TOK
    mode = "dense" if bias is None else "band"
    ins = [qT]
    in_specs = [pl.BlockSpec((1, GROUP, HEAD_DIM, TOK), lambda bi, kv, j: (bi, kv, 0, j))]
    if mode == "dense":
        ins += [k, vT]
        in_specs += [
            pl.BlockSpec((1, 1, nt, TOK, HEAD_DIM), lambda bi, kv, j: (bi, kv, 0, 0, 0)),
            pl.BlockSpec((1, 1, nt, VT_ROWS, TOK), lambda bi, kv, j: (bi, kv, 0, 0, 0)),
        ]
        bias_heads = 0
    else:
        def base(j):
            return jnp.clip(j - 1, 0, nt - 3)
        ins += [k, k, k, vT, vT, vT, bias]
        for slot in range(3):
            in_specs.append(pl.BlockSpec(
                (1, 1, 1, TOK, HEAD_DIM),
                lambda bi, kv, j, slot=slot: (bi, kv, base(j) + slot, 0, 0)))
        for slot in range(3):
            in_specs.append(pl.BlockSpec(
                (1, 1, 1, VT_ROWS, TOK),
                lambda bi, kv, j, slot=slot: (bi, kv, base(j) + slot, 0, 0)))
        bias_heads = 1 if bias.shape[1] == 1 else GROUP

        def variant(j):
            return (j > 0).astype(I32) + (j == nt - 1).astype(I32)
        if bias_heads == 1:
            in_specs.append(pl.BlockSpec((1, 1, 3, TOK, TOK), lambda bi, kv, j: (variant(j), 0, 0, 0, 0)))
        else:
            in_specs.append(pl.BlockSpec((1, GROUP, 3, TOK, TOK), lambda bi, kv, j: (variant(j), kv, 0, 0, 0)))
    if ctx is not None:
        kc, vcT = ctx
        nc = kc.shape[2]
        ins += [kc, vcT]
        in_specs += [
            pl.BlockSpec((1, 1, nc, TOK, HEAD_DIM), lambda bi, kv, j: (bi, kv, 0, 0, 0)),
            pl.BlockSpec((1, 1, nc, VT_ROWS, TOK), lambda bi, kv, j: (bi, kv, 0, 0, 0)),
        ]
    if sink is not None:
        ins.append(sink)
        in_specs.append(pl.BlockSpec(memory_space=pltpu.SMEM))
    return pl.pallas_call(
        functools.partial(_attn_kernel, mode=mode, n_local=nt, has_ctx=ctx is not None,
                          has_sink=sink is not None, bias_heads=bias_heads),
        grid=(b, N_KV_HEADS, nt),
        in_specs=in_specs,
        out_specs=pl.BlockSpec((1, GROUP, HEAD_DIM, TOK), lambda bi, kv, j: (bi, kv, 0, j)),
        out_shape=jax.ShapeDtypeStruct((b, N_HEADS, HEAD_DIM, n), BF16),
        scratch_shapes=[
            pltpu.VMEM((GROUP, 1, TOK), F32),
            pltpu.VMEM((GROUP, VT_ROWS, TOK), F32),
            pltpu.VMEM((N_SBUF, GROUP, TOK, TOK), F32),
        ],
        compiler_params=_cparams(("arbitrary", "arbitrary", "arbitrary")),
        name="attention_" + mode,
    )(*ins)


def _post_attn_kernel(oT_ref, x_ref, mod_ref, g_ref, wo_ref, wrT_ref, xm_ref, h2_ref, affT_ref, wo_s,
                      *, subtiles):
    @pl.when((pl.program_id(0) == 0) & (pl.program_id(1) == 0))
    def _():
        wo_s[...] = wo_ref[0].astype(BF16)

    for st in range(subtiles):
        tsl = slice(st * TOK, (st + 1) * TOK)
        oT = oT_ref[0, :, :, tsl].reshape(Q_DIM, TOK)
        y = _dot(oT, wo_s[...], _TN)
        xm = x_ref[0, tsl, :] + mod_ref[0, 2:3, :] * y
        xm_ref[0, tsl, :] = xm
        h2 = _mod_rmsnorm(xm, g_ref[...], mod_ref[0, 3:4, :], mod_ref[0, 4:5, :])
        h2_ref[0, tsl, :] = h2
        logitsT = _dot_f32(wrT_ref[...], h2, _NT)
        mx = jnp.max(logitsT, axis=0, keepdims=True)
        e = jnp.exp(logitsT - mx)
        affT_ref[0, :, tsl] = e / jnp.sum(e, axis=0, keepdims=True)


def _post_attn(oT, x, mod, mod_row, g, w_o, layer, w_router):
    b, n, _ = x.shape
    nt = n // TOK
    sub = SUBTILES if nt % SUBTILES == 0 else 1
    tok_spec = pl.BlockSpec((1, sub * TOK, D_MODEL), lambda bi, t: (bi, t, 0))
    return pl.pallas_call(
        functools.partial(_post_attn_kernel, subtiles=sub),
        grid=(b, nt // sub),
        in_specs=[
            pl.BlockSpec((1, N_HEADS, HEAD_DIM, sub * TOK), lambda bi, t: (bi, 0, 0, t)),
            tok_spec,
            pl.BlockSpec((1, 6, D_MODEL), lambda bi, t: (mod_row(bi), 0, 0)),
            pl.BlockSpec((1, D_MODEL), lambda bi, t: (0, 0)),
            pl.BlockSpec((1, Q_DIM, D_MODEL), lambda bi, t: (layer, 0, 0)),
            pl.BlockSpec((N_EXPERTS, D_MODEL), lambda bi, t: (0, 0)),
        ],
        out_specs=[tok_spec, tok_spec, pl.BlockSpec((1, N_EXPERTS, sub * TOK), lambda bi, t: (bi, 0, t))],
        out_shape=[
            jax.ShapeDtypeStruct((b, n, D_MODEL), F32),
            jax.ShapeDtypeStruct((b, n, D_MODEL), F32),
            jax.ShapeDtypeStruct((b, N_EXPERTS, n), F32),
        ],
        scratch_shapes=[pltpu.VMEM((Q_DIM, D_MODEL), BF16)],
        compiler_params=_cparams(("arbitrary", "arbitrary")),
        name="post_attn",
    )(oT, x, mod, g.reshape(1, D_MODEL), w_o, w_router.T)


def _select_kernel(ag_ref, a_ref, idx_ref, gate_ref, key_s, cx_s, ci_s, *, groups, chunks, cap):
    rows = groups * chunks
    slots = max(cap, LANES)
    a = a_ref[0]

    ri = lax.broadcasted_iota(I32, (rows, rows), 0)
    ci = lax.broadcasted_iota(I32, (rows, rows), 1)
    same = (ri // chunks) == (ci // chunks)
    blockdiag = jnp.where(same, 1.0, 0.0).astype(BF16)
    lower = jnp.where(same & (ci < ri), 1.0, 0.0).astype(BF16)
    li = lax.broadcasted_iota(I32, (LANES, LANES), 0)
    lj = lax.broadcasted_iota(I32, (LANES, LANES), 1)
    upper_incl = jnp.where(li <= lj, 1.0, 0.0).astype(BF16)
    ones_ll = jnp.ones((LANES, LANES), BF16)

    def as_bf16(mask):
        return jnp.where(mask, 1.0, 0.0).astype(BF16)

    def row_total(x01):
        return _dot(x01, ones_ll)

    def expert_total(x01):
        return _dot(blockdiag, row_total(x01).astype(BF16))

    capf = float(cap)

    ag = ag_ref[0]

    def search(step, bounds):
        lo, hi = bounds
        mid = 0.5 * (lo + hi)
        ok = jnp.sum(jnp.where(ag >= mid, 1.0, 0.0), axis=1, keepdims=True) >= capf
        return jnp.where(ok, mid, lo), jnp.where(ok, hi, mid)

    lo_g, hi_g = lax.fori_loop(0, BISECT_STEPS, search,
                               (jnp.zeros((groups, 1), F32), jnp.full((groups, 1), 2.0, F32)))
    to_rows = as_bf16(lax.broadcasted_iota(I32, (rows, groups), 0) // chunks
                      == lax.broadcasted_iota(I32, (rows, groups), 1))
    lo = _dot_exact_rhs(to_rows, jnp.broadcast_to(lo_g, (groups, LANES)))
    hi = _dot_exact_rhs(to_rows, jnp.broadcast_to(hi_g, (groups, LANES)))

    gt = a >= hi
    eq = (a >= lo) & (a < hi)
    need = capf - expert_total(as_bf16(gt))
    eq01 = as_bf16(eq)
    eq_rank = (_dot(eq01, upper_incl) - eq01.astype(F32)
               + _dot(lower, row_total(eq01).astype(BF16)))
    sel = gt | (eq & (eq_rank < need))
    sel01 = as_bf16(sel)
    sel_rows = row_total(sel01)
    c_excl = _dot(lower, sel_rows.astype(BF16))
    c_incl = c_excl + sel_rows
    key = 2.0 * (_dot(sel01, upper_incl) + c_excl) - sel01.astype(F32)
    first_lane = as_bf16(lax.broadcasted_iota(I32, (8, LANES), 1) == 0)
    cx_s[...] = _dot_exact_rhs(first_lane, c_excl, _NT)
    ci_s[...] = _dot_exact_rhs(first_lane, c_incl, _NT)
    key_s[...] = key

    rb = min(rows, LANES)
    jcol = lax.broadcasted_iota(I32, (slots, rb), 0).astype(F32)
    rlane = lax.broadcasted_iota(I32, (slots, rb), 1)
    jl = lax.broadcasted_iota(I32, (slots, LANES), 0).astype(F32)
    lane8 = lax.broadcasted_iota(I32, (8, LANES), 1).astype(BF16)
    chunk8 = (lax.broadcasted_iota(I32, (8, rb), 1) % chunks).astype(BF16)
    ones8 = jnp.ones((8, LANES), BF16)

    def per_group(e, carry):
        r0 = pl.multiple_of(((e * chunks) // rb) * rb, rb)
        mine = ((rlane + r0) // chunks) == e
        c_excl_l = cx_s[0:1, pl.ds(r0, rb)]
        c_incl_l = ci_s[0:1, pl.ds(r0, rb)]
        onehot = as_bf16(mine & (c_excl_l <= jcol) & (jcol < c_incl_l))
        hit = _dot_exact_rhs(onehot, key_s[pl.ds(r0, rb), :]) == 2.0 * jl + 1.0
        hit01 = as_bf16(hit)
        tok = _dot(chunk8, onehot, _NT) * float(LANES) + _dot(lane8, hit01, _NT)
        gsel = jnp.where(hit, _dot_exact_rhs(onehot, a_ref[0, pl.ds(r0, rb), :]), 0.0)
        g = _dot_exact_rhs(ones8, gsel, _NT)
        idx_ref[0, pl.ds(e, 1), :] = tok[0:1, :cap].astype(I32)
        gate_ref[0, pl.ds(e, 1), :] = g[0:1, :cap]
        return carry

    lax.fori_loop(0, groups, per_group, 0, unroll=4 if slots == LANES else 2)


def _select(affT, cap):
    b, _, n = affT.shape
    chunks = n // LANES
    bstep = b if b * N_EXPERTS * chunks <= 512 else 1
    groups = bstep * N_EXPERTS
    rows = groups * chunks
    idx, gate = pl.pallas_call(
        functools.partial(_select_kernel, groups=groups, chunks=chunks, cap=cap),
        grid=(b // bstep,),
        in_specs=[pl.BlockSpec((1, groups, n), lambda bi: (bi, 0, 0)),
                  pl.BlockSpec((1, rows, LANES), lambda bi: (bi, 0, 0))],
        out_specs=[pl.BlockSpec((1, groups, cap), lambda bi: (bi, 0, 0))] * 2,
        out_shape=[jax.ShapeDtypeStruct((b // bstep, groups, cap), I32),
                   jax.ShapeDtypeStruct((b // bstep, groups, cap), F32)],
        scratch_shapes=[pltpu.VMEM((rows, LANES), F32), pltpu.VMEM((8, rows), F32), pltpu.VMEM((8, rows), F32)],
        compiler_params=_cparams(("arbitrary",)),
        name="select",
    )(affT.reshape(b // bstep, groups, n), affT.reshape(b // bstep, rows, LANES))
    return idx.reshape(b, N_EXPERTS, cap), gate.reshape(b, N_EXPERTS, cap)


def _dispatch_kernel(idx_ref, h_ref, xg_ref, buf, *, cap, eg):
    bi = pl.program_id(0)
    g0 = pl.program_id(1) * eg

    def per_expert(el, carry):
        base = ((bi * N_EXPERTS) + g0 + el) * cap

        def body(jo, c2):
            j0 = pl.multiple_of(jo * UNROLL, UNROLL)
            dst = buf.at[pl.ds(j0, UNROLL)]
            for u in range(UNROLL):
                t = idx_ref[base + j0 + u]
                dst[u:u + 1, :] = h_ref[0, pl.ds(t, 1), :]
            return c2

        lax.fori_loop(0, cap // UNROLL, body, 0)
        xg_ref[el, 0] = buf[...].astype(BF16)
        return carry

    lax.fori_loop(0, eg, per_expert, 0)


def _dispatch(idx, h2, cap, eg):
    b, n, _ = h2.shape
    grid_spec = pltpu.PrefetchScalarGridSpec(
        num_scalar_prefetch=1,
        grid=(b, N_EXPERTS // eg),
        in_specs=[pl.BlockSpec((1, n, D_MODEL), lambda bi, g, idx_r: (bi, 0, 0))],
        out_specs=pl.BlockSpec((eg, 1, cap, D_MODEL), lambda bi, g, idx_r: (g, bi, 0, 0)),
        scratch_shapes=[pltpu.VMEM((cap, D_MODEL), F32)],
    )
    return pl.pallas_call(
        functools.partial(_dispatch_kernel, cap=cap, eg=eg),
        grid_spec=grid_spec,
        out_shape=jax.ShapeDtypeStruct((N_EXPERTS, b, cap, D_MODEL), BF16),
        compiler_params=_cparams(("arbitrary", "arbitrary")),
        name="dispatch",
    )(idx.reshape(-1), h2)


def _combine_kernel(idx_ref, y_ref, acc_ref, *, cap, eg):
    bi = pl.program_id(0)
    g = pl.program_id(1)
    g0 = g * eg

    @pl.when(g == 0)
    def _():
        acc_ref[...] = jnp.zeros_like(acc_ref)

    def per_expert(el, carry):
        base = ((bi * N_EXPERTS) + g0 + el) * cap

        def body(jo, c2):
            j0 = pl.multiple_of(jo * UNROLL, UNROLL)
            toks = [idx_ref[base + j0 + u] for u in range(UNROLL)]
            olds = [acc_ref[0, pl.ds(t, 1), :] for t in toks]
            src = y_ref.at[el, 0, pl.ds(j0, UNROLL)]
            for u in range(UNROLL):
                acc_ref[0, pl.ds(toks[u], 1), :] = olds[u] + src[u:u + 1, :]
            return c2

        lax.fori_loop(0, cap // UNROLL, body, 0)
        return carry

    lax.fori_loop(0, eg, per_expert, 0)


def _combine(idx, y, n, cap, eg):
    b = y.shape[1]
    grid_spec = pltpu.PrefetchScalarGridSpec(
        num_scalar_prefetch=1,
        grid=(b, N_EXPERTS // eg),
        in_specs=[pl.BlockSpec((eg, 1, cap, D_MODEL), lambda bi, g, idx_r: (g, bi, 0, 0))],
        out_specs=pl.BlockSpec((1, n, D_MODEL), lambda bi, g, idx_r: (bi, 0, 0)),
    )
    return pl.pallas_call(
        functools.partial(_combine_kernel, cap=cap, eg=eg),
        grid_spec=grid_spec,
        out_shape=jax.ShapeDtypeStruct((b, n, D_MODEL), F32),
        compiler_params=_cparams(("arbitrary", "arbitrary")),
        name="combine",
    )(idx.reshape(-1), y)


def _ffn_kernel(xg_ref, gate_ref, wg_ref, wu_ref, wd_ref, y_ref, wg_s, wu_s, wd_s):
    @pl.when(pl.program_id(1) == 0)
    def _():
        wg_s[...] = wg_ref[0, 0].astype(BF16)
        wu_s[...] = wu_ref[0, 0].astype(BF16)
        wd_s[...] = wd_ref[0, 0].astype(BF16)

    rows = xg_ref.shape[1]
    for r0 in range(0, rows, FFN_SUB):
        rsl = slice(r0, r0 + FFN_SUB)
        x = xg_ref[0, rsl, :]
        a = _dot(x, wg_s[...])
        u = _dot(x, wu_s[...])
        hmid = (a * (1.0 / (1.0 + jnp.exp(-a))) * u).astype(BF16)
        y_ref[0, rsl, :] = _dot(hmid, wd_s[...]) * gate_ref[0, rsl, :]


def _ffn(xg, gate, w_gate, w_up, w_down, layer):
    _, r, _ = xg.shape
    tm = min(FFN_TM, r)
    w_spec = pl.BlockSpec((1, 1, D_MODEL, D_MODEL), lambda e, m: (layer, e, 0, 0))
    return pl.pallas_call(
        _ffn_kernel,
        grid=(N_EXPERTS, r // tm),
        in_specs=[
            pl.BlockSpec((1, tm, D_MODEL), lambda e, m: (e, m, 0)),
            pl.BlockSpec((1, tm, 1), lambda e, m: (e, m, 0)),
            w_spec, w_spec, w_spec,
        ],
        out_specs=pl.BlockSpec((1, tm, D_MODEL), lambda e, m: (e, m, 0)),
        out_shape=jax.ShapeDtypeStruct((N_EXPERTS, r, D_MODEL), F32),
        scratch_shapes=[pltpu.VMEM((D_MODEL, D_MODEL), BF16)] * 3,
        compiler_params=_cparams(("arbitrary", "arbitrary")),
        name="expert_ffn",
    )(xg, gate, w_gate, w_up, w_down)


def _moe(h2, affT, w_gate, w_up, w_down, layer, eg):
    b, n, _ = h2.shape
    cap = (CAPACITY_FACTOR * n) // N_EXPERTS
    idx, gate = _select(affT, cap)
    xg = _dispatch(idx, h2, cap, eg)
    gate_rows = jnp.swapaxes(gate, 0, 1).reshape(N_EXPERTS, b * cap, 1)
    y = _ffn(xg.reshape(N_EXPERTS, b * cap, D_MODEL), gate_rows, w_gate, w_up, w_down, layer)
    return _combine(idx, y.reshape(N_EXPERTS, b, cap, D_MODEL), n, cap, eg)


def _residual_kernel(x_ref, acc_ref, mod_ref, out_ref):
    out_ref[0] = x_ref[0] + mod_ref[0, 5:6, :] * acc_ref[0]


def _residual(x, acc, mod, mod_row):
    b, n, _ = x.shape
    tok_spec = pl.BlockSpec((1, TOK, D_MODEL), lambda bi, t: (bi, t, 0))
    return pl.pallas_call(
        _residual_kernel,
        grid=(b, n // TOK),
        in_specs=[tok_spec, tok_spec, pl.BlockSpec((1, 6, D_MODEL), lambda bi, t: (mod_row(bi), 0, 0))],
        out_specs=tok_spec,
        out_shape=jax.ShapeDtypeStruct((b, n, D_MODEL), F32),
        compiler_params=_cparams(("arbitrary", "arbitrary")),
        name="ffn_residual",
    )(x, acc, mod)


def _rope_tables(n):
    t = jnp.arange(n)
    inv_freq = ROPE_THETA ** (-jnp.arange(AXIS_PAIRS, dtype=F32) / AXIS_PAIRS)
    ang_r = (t // GRID_W).astype(F32)[None, :] * inv_freq[:, None]
    ang_c = (t % GRID_W).astype(F32)[None, :] * inv_freq[:, None]
    ang = jnp.concatenate([ang_r, ang_c], axis=0)
    return jnp.cos(ang), jnp.sin(ang)


def _band_chunk_starts(nt):
    return (0, None, nt - 3)


def _window_bias(nt):
    u = jnp.arange(TOK)[:, None]
    q = jnp.arange(TOK)[None, :]
    variants = []
    for rel0 in (0, -1, -2):
        slots = []
        for slot in range(3):
            d = (rel0 + slot) * TOK + u - q
            slots.append(jnp.where(jnp.abs(d) <= WINDOW, 0.0, NEG_INF).astype(F32))
        variants.append(jnp.stack(slots))
    return jnp.stack(variants)[:, None]


def _neighborhood_bias(rpb, rows):
    nt = rows // ROWS_PER_TOK
    kc = jnp.arange(GRID_W)[:, None]
    qc = jnp.arange(GRID_W)[None, :]
    cs = jnp.clip(qc - KW // 2, 0, GRID_W - KW)
    col_ok = (kc >= cs) & (kc < cs + KW)
    diag = (kc - qc + (KW - 1))[None] == jnp.arange(2 * KW - 1)[:, None, None]
    toep = jnp.einsum("hdj,jkq->hdkq", rpb.astype(F32) * LOG2E, diag.astype(F32),
                      precision=lax.Precision.HIGHEST)
    toep = jnp.where(col_ok, toep, NEG_INF)
    masked = jnp.full((rpb.shape[0], GRID_W, GRID_W), NEG_INF, F32)
    variants = []
    for j, chunk0 in ((0, 0), (1, 0), (nt - 1, nt - 3)):
        slots = []
        for slot in range(3):
            block_rows = []
            for a in range(ROWS_PER_TOK):
                kr = (chunk0 + slot) * ROWS_PER_TOK + a
                blocks = []
                for b in range(ROWS_PER_TOK):
                    qr = j * ROWS_PER_TOK + b
                    rs = min(max(qr - KH // 2, 0), rows - KH)
                    ok = rs <= kr < rs + KH
                    blocks.append(toep[:, kr - qr + (KH - 1)] if ok else masked)
                block_rows.append(jnp.concatenate(blocks, axis=-1))
            slots.append(jnp.concatenate(block_rows, axis=-2))
        variants.append(jnp.stack(slots, axis=1))
    return jnp.stack(variants)


def kernel(x_prompt, x_sample, cache_k, cache_v, c, c_ctx, w_ada, b_ada, norm_attn_g, norm_ffn_g,
           w_qkv, q_norm_g, k_norm_g, w_o, sink_logits, rel_pos_bias, w_router, w_gate, w_up, w_down):
    bp, n_p, _ = x_prompt.shape
    bs, n_s, _ = x_sample.shape
    past = cache_k.shape[2]
    assert n_p % TOK == 0 and n_s % TOK == 0 and past % TOK == 0
    assert n_s // TOK >= 3 and 1 + bs <= COND_ROWS

    cond = jnp.zeros((COND_ROWS, D_MODEL), F32).at[0].set(c_ctx).at[1:1 + bs].set(c)
    mods = _ada_modulation(cond, w_ada, b_ada).reshape(DEPTH, COND_ROWS, 6, D_MODEL)
    row_p = lambda bi: 0
    row_s = lambda bi: bi + 1

    nc = past // TOK
    kc_all = jnp.transpose(cache_k, (1, 0, 3, 2, 4)).astype(BF16)
    kc_all = kc_all.reshape(DEPTH, bs, N_KV_HEADS, nc, TOK, HEAD_DIM)
    vc_all = jnp.transpose(cache_v, (1, 0, 3, 4, 2)).astype(BF16)
    vc_all = jnp.transpose(vc_all.reshape(DEPTH, bs, N_KV_HEADS, HEAD_DIM, nc, TOK), (0, 1, 2, 4, 3, 5))
    vc_all = jnp.concatenate(
        [vc_all, jnp.ones((DEPTH, bs, N_KV_HEADS, nc, VT_ROWS - HEAD_DIM, TOK), BF16)], axis=4)

    rope_tabs = _rope_tables(n_s)
    win_bias = _window_bias(n_s // TOK)

    xp, xs = x_prompt, x_sample
    acc_p = acc_s = None
    new_k, new_v = [], []
    for i in range(DEPTH):
        kind = i % 3
        sink = sink_logits[i // 3] if kind == 1 else None
        prev = mods[i - 1] if i > 0 else None

        xp, qT, k, vT, ck, cv = _pre_attn(xp, acc_p, prev, mods[i], row_p, norm_attn_g[i], w_qkv, i,
                                          q_norm_g[i], k_norm_g[i], None, True)
        new_k.append(ck.reshape(bp, n_p, N_KV_HEADS, HEAD_DIM))
        new_v.append(cv.reshape(bp, n_p, N_KV_HEADS, HEAD_DIM))
        oT = _attention(qT, k, vT, None, sink, None)
        xp, h2, affT = _post_attn(oT, xp, mods[i], row_p, norm_ffn_g[i], w_o, i, w_router[i])
        acc_p = _moe(h2, affT, w_gate, w_up, w_down, i, N_EXPERTS)

        tabs = rope_tabs if kind != 2 else None
        xs, qT, k, vT = _pre_attn(xs, acc_s, prev, mods[i], row_s, norm_attn_g[i], w_qkv, i,
                                  q_norm_g[i], k_norm_g[i], tabs, False)
        ctx = (kc_all[i], vc_all[i])
        if kind == 0:
            oT = _attention(qT, k, vT, ctx, None, None)
        elif kind == 1:
            oT = _attention(qT, k, vT, ctx, sink, win_bias)
        else:
            oT = _attention(qT, k, vT, ctx, None, _neighborhood_bias(rel_pos_bias[i // 3], n_s // GRID_W))
        xs, h2, affT = _post_attn(oT, xs, mods[i], row_s, norm_ffn_g[i], w_o, i, w_router[i])
        acc_s = _moe(h2, affT, w_gate, w_up, w_down, i, 1)

    y_prompt = _residual(xp, acc_p, mods[DEPTH - 1], row_p)
    y_sample = _residual(xs, acc_s, mods[DEPTH - 1], row_s)
    return (y_prompt, y_sample, jnp.stack(new_k, axis=1), jnp.stack(new_v, axis=1))
```

```python
import functools

import jax
import jax.numpy as jnp
from jax import lax
from jax.experimental import pallas as pl
from jax.experimental.pallas import tpu as pltpu

F32 = jnp.float32
BF16 = jnp.bfloat16
I32 = jnp.int32

D_MODEL = 1024
DEPTH = 4
GRID_W = 64
N_HEADS = 16
N_KV_HEADS = 4
GROUP = N_HEADS // N_KV_HEADS
HEAD_DIM = 64
AXIS_PAIRS = HEAD_DIM // 4
ROPE_THETA = 10000.0
QKV_DIM = (N_HEADS + 2 * N_KV_HEADS) * HEAD_DIM
Q_DIM = N_HEADS * HEAD_DIM
KV_DIM = N_KV_HEADS * HEAD_DIM
WINDOW = 128
KH = 8
KW = 16
N_EXPERTS = 16
CAPACITY_FACTOR = 2
SCALE = HEAD_DIM ** -0.5
LOG2E = 1.4426950408889634
NEG_INF = -1e30
EPS = 1e-6

LANES = 128
TOK = 256
VT_ROWS = 64 + 16
AHEAD = 2
N_SBUF = 4
SUBTILES = 4
ROWS_PER_TOK = TOK // GRID_W
COND_ROWS = 8
FFN_TM = 512
FFN_SUB = 256
UNROLL = 8
BISECT_STEPS = 64
VMEM_LIMIT = 56 * 1024 * 1024


def _cparams(sem):
    return pltpu.CompilerParams(dimension_semantics=sem, vmem_limit_bytes=VMEM_LIMIT)


def _split3(x):
    hi = x.astype(BF16)
    r = x - hi.astype(F32)
    mid = r.astype(BF16)
    lo = (r - mid.astype(F32)).astype(BF16)
    return hi, mid, lo


def _dot(a, b, dims=(((1,), (0,)), ((), ()))):
    return lax.dot_general(a, b, dims, preferred_element_type=F32)


_NT = (((1,), (1,)), ((), ()))
_TN = (((0,), (0,)), ((), ()))


def _dot_exact_rhs(onehot_bf16, x_f32, dims=(((1,), (0,)), ((), ()))):
    hi, mid, lo = _split3(x_f32)
    return _dot(onehot_bf16, hi, dims) + _dot(onehot_bf16, mid, dims) + _dot(onehot_bf16, lo, dims)


def _dot_f32(a, b, dims):
    a_hi, a_mid, _ = _split3(a)
    b_hi, b_mid, _ = _split3(b)
    return (_dot(a_hi, b_hi, dims) + _dot(a_hi, b_mid, dims) + _dot(a_mid, b_hi, dims))


def _mod_rmsnorm(x, g, shift, scale):
    ms = jnp.mean(x * x, axis=-1, keepdims=True)
    return (x * lax.rsqrt(ms + EPS) * g) * (1.0 + scale) + shift


ADA_TN = 1536


def _ada_kernel(cond_ref, w_ref, b_ref, out_ref):
    cnd = cond_ref[...]
    act = cnd * (1.0 / (1.0 + jnp.exp(-cnd)))
    out_ref[0] = _dot_f32(act, w_ref[0], (((1,), (0,)), ((), ()))) + b_ref[0]


def _ada_modulation(cond, w_ada, b_ada):
    n_out = 6 * D_MODEL
    return pl.pallas_call(
        _ada_kernel,
        grid=(DEPTH, n_out // ADA_TN),
        in_specs=[
            pl.BlockSpec((COND_ROWS, D_MODEL), lambda i, n: (0, 0)),
            pl.BlockSpec((1, D_MODEL, ADA_TN), lambda i, n: (i, 0, n)),
            pl.BlockSpec((1, 1, ADA_TN), lambda i, n: (i, 0, n)),
        ],
        out_specs=pl.BlockSpec((1, COND_ROWS, ADA_TN), lambda i, n: (i, 0, n)),
        out_shape=jax.ShapeDtypeStruct((DEPTH, COND_ROWS, n_out), F32),
        compiler_params=_cparams(("arbitrary", "arbitrary")),
        name="ada_modulation",
    )(cond, w_ada, b_ada.reshape(DEPTH, 1, n_out))


def _pre_attn_kernel(*refs, fuse_residual, rope, emit_cache, subtiles, batch_tiles):
    it = iter(refs)
    x_ref = next(it)
    if fuse_residual:
        acc_ref = next(it)
        pmod_ref = next(it)
    mod_ref = next(it)
    g_ref = next(it)
    w_ref = next(it)
    gq_ref = next(it)
    gk_ref = next(it)
    if rope:
        cos_ref = next(it)
        sin_ref = next(it)
    if fuse_residual:
        xo_ref = next(it)
    qT_ref = next(it)
    k_ref = next(it)
    vT_ref = next(it)
    if emit_cache:
        ck_ref = next(it)
        cv_ref = next(it)
    w_s = next(it)

    @pl.when((pl.program_id(0) == 0) & (pl.program_id(1) == 0))
    def _():
        w_s[...] = w_ref[0].astype(BF16)

    def head_norm(blk, gain):
        ss = jnp.sum(blk * blk, axis=0, keepdims=True)
        return blk * lax.rsqrt(ss * (1.0 / HEAD_DIM) + EPS) * gain

    def rotate(blk, tsl):
        if not rope:
            return blk
        cs = cos_ref[:, tsl]
        sn = sin_ref[:, tsl]
        parts = []
        for a in range(2):
            x1 = blk[32 * a:32 * a + 16]
            x2 = blk[32 * a + 16:32 * a + 32]
            c = cs[16 * a:16 * a + 16]
            s = sn[16 * a:16 * a + 16]
            parts.append(x1 * c - x2 * s)
            parts.append(x2 * c + x1 * s)
        return jnp.concatenate(parts, axis=0)

    gq = gq_ref[...]
    gk = gk_ref[...]
    for bt, st in [(bt, st) for bt in range(batch_tiles) for st in range(subtiles)]:
        tsl = slice(st * TOK, (st + 1) * TOK)
        x = x_ref[bt, tsl, :]
        if fuse_residual:
            x = x + pmod_ref[0, 5:6, :] * acc_ref[bt, tsl, :]
            xo_ref[bt, tsl, :] = x
        h = _mod_rmsnorm(x, g_ref[...], mod_ref[0, 0:1, :], mod_ref[0, 1:2, :])
        qkv = _dot(h.astype(BF16), w_s[...])
        if emit_cache:
            cv_ref[bt, tsl, :] = qkv[:, Q_DIM + KV_DIM:]
        qkvT = qkv.T
        for hd in range(N_HEADS):
            blk = qkvT[HEAD_DIM * hd:HEAD_DIM * (hd + 1)]
            qT_ref[bt, hd, :, tsl] = (rotate(head_norm(blk, gq), tsl) * (SCALE * LOG2E)).astype(BF16)
        kn_all = []
        for kv in range(N_KV_HEADS):
            off = Q_DIM + HEAD_DIM * kv
            kn_all.append(rotate(head_norm(qkvT[off:off + HEAD_DIM], gk), tsl))
            offv = Q_DIM + KV_DIM + HEAD_DIM * kv
            vT_ref[bt, kv, st, :HEAD_DIM] = qkvT[offv:offv + HEAD_DIM].astype(BF16)
            vT_ref[bt, kv, st, HEAD_DIM:] = jnp.ones((VT_ROWS - HEAD_DIM, TOK), BF16)
        kn_tok = jnp.concatenate(kn_all, axis=0).T
        for kv in range(N_KV_HEADS):
            k_ref[bt, kv, st] = kn_tok[:, HEAD_DIM * kv:HEAD_DIM * (kv + 1)].astype(BF16)
        if emit_cache:
            ck_ref[bt, tsl, :] = kn_tok


def _step_tiles(b, nt, shared_modulation):
    if nt % SUBTILES == 0:
        return SUBTILES, 1
    if shared_modulation and b % SUBTILES == 0:
        return 1, SUBTILES
    return 1, 1


def _pre_attn(x, acc, prev_mod, mod, mod_row, g, w_qkv, layer, gq, gk, rope_tabs, emit_cache):
    b, n, _ = x.shape
    nt = n // TOK
    sub, bsub = _step_tiles(b, nt, emit_cache)
    fuse = acc is not None
    rope = rope_tabs is not None
    tok_spec = pl.BlockSpec((bsub, sub * TOK, D_MODEL), lambda bi, t: (bi, t, 0))
    mod_spec = pl.BlockSpec((1, 6, D_MODEL), lambda bi, t: (mod_row(bi), 0, 0))
    ins, in_specs = [x], [tok_spec]
    if fuse:
        ins += [acc, prev_mod]
        in_specs += [tok_spec, mod_spec]
    ins += [mod, g.reshape(1, D_MODEL), w_qkv,
            jnp.broadcast_to(gq[:, None], (HEAD_DIM, TOK)),
            jnp.broadcast_to(gk[:, None], (HEAD_DIM, TOK))]
    in_specs += [
        mod_spec,
        pl.BlockSpec((1, D_MODEL), lambda bi, t: (0, 0)),
        pl.BlockSpec((1, D_MODEL, QKV_DIM), lambda bi, t: (layer, 0, 0)),
        pl.BlockSpec((HEAD_DIM, TOK), lambda bi, t: (0, 0)),
        pl.BlockSpec((HEAD_DIM, TOK), lambda bi, t: (0, 0)),
    ]
    if rope:
        ins += list(rope_tabs)
        in_specs += [pl.BlockSpec((2 * AXIS_PAIRS, sub * TOK), lambda bi, t: (0, t))] * 2
    out_shape, out_specs = [], []
    if fuse:
        out_shape.append(jax.ShapeDtypeStruct((b, n, D_MODEL), F32))
        out_specs.append(tok_spec)
    out_shape += [
        jax.ShapeDtypeStruct((b, N_HEADS, HEAD_DIM, n), BF16),
        jax.ShapeDtypeStruct((b, N_KV_HEADS, nt, TOK, HEAD_DIM), BF16),
        jax.ShapeDtypeStruct((b, N_KV_HEADS, nt, VT_ROWS, TOK), BF16),
    ]
    out_specs += [
        pl.BlockSpec((bsub, N_HEADS, HEAD_DIM, sub * TOK), lambda bi, t: (bi, 0, 0, t)),
        pl.BlockSpec((bsub, N_KV_HEADS, sub, TOK, HEAD_DIM), lambda bi, t: (bi, 0, t, 0, 0)),
        pl.BlockSpec((bsub, N_KV_HEADS, sub, VT_ROWS, TOK), lambda bi, t: (bi, 0, t, 0, 0)),
    ]
    if emit_cache:
        out_shape += [jax.ShapeDtypeStruct((b, n, KV_DIM), F32)] * 2
        out_specs += [pl.BlockSpec((bsub, sub * TOK, KV_DIM), lambda bi, t: (bi, t, 0))] * 2
    outs = pl.pallas_call(
        functools.partial(_pre_attn_kernel, fuse_residual=fuse, rope=rope, emit_cache=emit_cache,
                          subtiles=sub, batch_tiles=bsub),
        grid=(b // bsub, nt // sub),
        in_specs=in_specs,
        out_specs=out_specs,
        out_shape=out_shape,
        scratch_shapes=[pltpu.VMEM((D_MODEL, QKV_DIM), BF16)],
        compiler_params=_cparams(("arbitrary", "arbitrary")),
        name="pre_attn",
    )(*ins)
    outs = list(outs)
    x_new = outs.pop(0) if fuse else x
    return [x_new] + outs


def _attn_kernel(*refs, mode, n_local, has_ctx, has_sink, bias_heads):
    it = iter(refs)
    qT_ref = next(it)
    if mode == "dense":
        k_ref = next(it)
        vT_ref = next(it)
    else:
        k_refs = [next(it) for _ in range(3)]
        vT_refs = [next(it) for _ in range(3)]
        bias_ref = next(it)
    if has_ctx:
        kc_ref = next(it)
        vcT_ref = next(it)
    if has_sink:
        sink_ref = next(it)
    oT_ref = next(it)
    m_s = next(it)
    acc_s = next(it)
    s_buf = next(it)

    kv = pl.program_id(1)
    for hd in range(GROUP):
        acc_s[hd, :HEAD_DIM] = jnp.zeros((HEAD_DIM, TOK), F32)
        if has_sink:
            m_s[hd] = jnp.full((1, TOK), sink_ref[kv * GROUP + hd] * LOG2E, F32)
            acc_s[hd, HEAD_DIM:] = jnp.ones((VT_ROWS - HEAD_DIM, TOK), F32)
        else:
            m_s[hd] = jnp.full((1, TOK), NEG_INF, F32)
            acc_s[hd, HEAD_DIM:] = jnp.zeros((VT_ROWS - HEAD_DIM, TOK), F32)

    def scores(par, kch):
        for hd in range(GROUP):
            s_buf[par, hd] = _dot(kch, qT_ref[0, hd])

    def consume(par, vch, bias_of_head):
        for hd in range(GROUP):
            s = s_buf[par, hd]
            if bias_of_head is not None:
                s = s + bias_of_head(hd)
            m_prev = m_s[hd]
            m_new = jnp.maximum(m_prev, jnp.max(s, axis=0, keepdims=True))
            alpha = jnp.exp2(m_prev - m_new)
            p = jnp.exp2(s - m_new)
            acc_s[hd] = alpha * acc_s[hd] + _dot(vch, p.astype(BF16))
            m_s[hd] = m_new

    static_chunks = []
    if mode == "band":
        for slot in range(3):
            if bias_heads == 1:
                bias_fn = lambda hd, slot=slot: bias_ref[0, 0, slot]
            else:
                bias_fn = lambda hd, slot=slot: bias_ref[0, hd, slot]
            static_chunks.append((k_refs[slot].at[0, 0, 0], vT_refs[slot].at[0, 0, 0], bias_fn))
    if has_ctx:
        for c in range(kc_ref.shape[2]):
            static_chunks.append((kc_ref.at[0, 0, c], vcT_ref.at[0, 0, c], None))

    n_loop = 0
    if mode == "dense":
        n_loop = max(0, (n_local - AHEAD) // N_SBUF) * N_SBUF
        static_chunks = ([(k_ref.at[0, 0, c], vT_ref.at[0, 0, c], None) for c in range(n_loop, n_local)]
                         + static_chunks)
    for c in range(AHEAD):
        if c < n_loop:
            scores(c % N_SBUF, k_ref[0, 0, c])
        elif c - n_loop < len(static_chunks):
            scores(c % N_SBUF, static_chunks[c - n_loop][0][...])
    if n_loop:
        def body(i, carry):
            c0 = N_SBUF * i
            for u in range(N_SBUF):
                scores((u + AHEAD) % N_SBUF, k_ref[0, 0, c0 + u + AHEAD])
                consume(u, vT_ref[0, 0, c0 + u], None)
            return carry
        lax.fori_loop(0, n_loop // N_SBUF, body, 0)
    for i, (_, v_at, bias_fn) in enumerate(static_chunks):
        if i + AHEAD < len(static_chunks):
            scores((n_loop + i + AHEAD) % N_SBUF, static_chunks[i + AHEAD][0][...])
        consume((n_loop + i) % N_SBUF, v_at[...], bias_fn)

    for hd in range(GROUP):
        oT_ref[0, hd] = (acc_s[hd, :HEAD_DIM] / acc_s[hd, HEAD_DIM:HEAD_DIM + 1]).astype(BF16)


def _attention(qT, k, vT, ctx, sink, bias):
    b, _, _, n = qT.shape
    nt = n // TOK
    mode = "dense" if bias is None else "band"
    ins = [qT]
    in_specs = [pl.BlockSpec((1, GROUP, HEAD_DIM, TOK), lambda bi, kv, j: (bi, kv, 0, j))]
    if mode == "dense":
        ins += [k, vT]
        in_specs += [
            pl.BlockSpec((1, 1, nt, TOK, HEAD_DIM), lambda bi, kv, j: (bi, kv, 0, 0, 0)),
            pl.BlockSpec((1, 1, nt, VT_ROWS, TOK), lambda bi, kv, j: (bi, kv, 0, 0, 0)),
        ]
        bias_heads = 0
    else:
        def base(j):
            return jnp.clip(j - 1, 0, nt - 3)
        ins += [k, k, k, vT, vT, vT, bias]
        for slot in range(3):
            in_specs.append(pl.BlockSpec(
                (1, 1, 1, TOK, HEAD_DIM),
                lambda bi, kv, j, slot=slot: (bi, kv, base(j) + slot, 0, 0)))
        for slot in range(3):
            in_specs.append(pl.BlockSpec(
                (1, 1, 1, VT_ROWS, TOK),
                lambda bi, kv, j, slot=slot: (bi, kv, base(j) + slot, 0, 0)))
        bias_heads = 1 if bias.shape[1] == 1 else GROUP

        def variant(j):
            return (j > 0).astype(I32) + (j == nt - 1).astype(I32)
        if bias_heads == 1:
            in_specs.append(pl.BlockSpec((1, 1, 3, TOK, TOK), lambda bi, kv, j: (variant(j), 0, 0, 0, 0)))
        else:
            in_specs.append(pl.BlockSpec((1, GROUP, 3, TOK, TOK), lambda bi, kv, j: (variant(j), kv, 0, 0, 0)))
    if ctx is not None:
        kc, vcT = ctx
        nc = kc.shape[2]
        ins += [kc, vcT]
        in_specs += [
            pl.BlockSpec((1, 1, nc, TOK, HEAD_DIM), lambda bi, kv, j: (bi, kv, 0, 0, 0)),
            pl.BlockSpec((1, 1, nc, VT_ROWS, TOK), lambda bi, kv, j: (bi, kv, 0, 0, 0)),
        ]
    if sink is not None:
        ins.append(sink)
        in_specs.append(pl.BlockSpec(memory_space=pltpu.SMEM))
    return pl.pallas_call(
        functools.partial(_attn_kernel, mode=mode, n_local=nt, has_ctx=ctx is not None,
                          has_sink=sink is not None, bias_heads=bias_heads),
        grid=(b, N_KV_HEADS, nt),
        in_specs=in_specs,
        out_specs=pl.BlockSpec((1, GROUP, HEAD_DIM, TOK), lambda bi, kv, j: (bi, kv, 0, j)),
        out_shape=jax.ShapeDtypeStruct((b, N_HEADS, HEAD_DIM, n), BF16),
        scratch_shapes=[
            pltpu.VMEM((GROUP, 1, TOK), F32),
            pltpu.VMEM((GROUP, VT_ROWS, TOK), F32),
            pltpu.VMEM((N_SBUF, GROUP, TOK, TOK), F32),
        ],
        compiler_params=_cparams(("arbitrary", "arbitrary", "arbitrary")),
        name="attention_" + mode,
    )(*ins)


def _post_attn_kernel(oT_ref, x_ref, mod_ref, g_ref, wo_ref, wrT_ref, xm_ref, h2_ref, affT_ref, wo_s,
                      *, subtiles, batch_tiles):
    @pl.when((pl.program_id(0) == 0) & (pl.program_id(1) == 0))
    def _():
        wo_s[...] = wo_ref[0].astype(BF16)

    for bt, st in [(bt, st) for bt in range(batch_tiles) for st in range(subtiles)]:
        tsl = slice(st * TOK, (st + 1) * TOK)
        oT = oT_ref[bt, :, :, tsl].reshape(Q_DIM, TOK)
        y = _dot(oT, wo_s[...], _TN)
        xm = x_ref[bt, tsl, :] + mod_ref[0, 2:3, :] * y
        xm_ref[bt, tsl, :] = xm
        h2 = _mod_rmsnorm(xm, g_ref[...], mod_ref[0, 3:4, :], mod_ref[0, 4:5, :])
        h2_ref[bt, tsl, :] = h2
        logitsT = _dot_f32(wrT_ref[...], h2, _NT)
        mx = jnp.max(logitsT, axis=0, keepdims=True)
        e = jnp.exp(logitsT - mx)
        affT_ref[bt, :, tsl] = e / jnp.sum(e, axis=0, keepdims=True)


def _post_attn(oT, x, mod, mod_row, g, w_o, layer, w_router, shared_modulation):
    b, n, _ = x.shape
    nt = n // TOK
    sub, bsub = _step_tiles(b, nt, shared_modulation)
    tok_spec = pl.BlockSpec((bsub, sub * TOK, D_MODEL), lambda bi, t: (bi, t, 0))
    return pl.pallas_call(
        functools.partial(_post_attn_kernel, subtiles=sub, batch_tiles=bsub),
        grid=(b // bsub, nt // sub),
        in_specs=[
            pl.BlockSpec((bsub, N_HEADS, HEAD_DIM, sub * TOK), lambda bi, t: (bi, 0, 0, t)),
            tok_spec,
            pl.BlockSpec((1, 6, D_MODEL), lambda bi, t: (mod_row(bi), 0, 0)),
            pl.BlockSpec((1, D_MODEL), lambda bi, t: (0, 0)),
            pl.BlockSpec((1, Q_DIM, D_MODEL), lambda bi, t: (layer, 0, 0)),
            pl.BlockSpec((N_EXPERTS, D_MODEL), lambda bi, t: (0, 0)),
        ],
        out_specs=[tok_spec, tok_spec, pl.BlockSpec((bsub, N_EXPERTS, sub * TOK), lambda bi, t: (bi, 0, t))],
        out_shape=[
            jax.ShapeDtypeStruct((b, n, D_MODEL), F32),
            jax.ShapeDtypeStruct((b, n, D_MODEL), F32),
            jax.ShapeDtypeStruct((b, N_EXPERTS, n), F32),
        ],
        scratch_shapes=[pltpu.VMEM((Q_DIM, D_MODEL), BF16)],
        compiler_params=_cparams(("arbitrary", "arbitrary")),
        name="post_attn",
    )(oT, x, mod, g.reshape(1, D_MODEL), w_o, w_router.T)


def _select_kernel(ag_ref, a_ref, idx_ref, gate_ref, key_s, cx_s, ci_s, *, groups, chunks, cap):
    rows = groups * chunks
    slots = max(cap, LANES)
    a = a_ref[0]

    ri = lax.broadcasted_iota(I32, (rows, rows), 0)
    ci = lax.broadcasted_iota(I32, (rows, rows), 1)
    same = (ri // chunks) == (ci // chunks)
    blockdiag = jnp.where(same, 1.0, 0.0).astype(BF16)
    lower = jnp.where(same & (ci < ri), 1.0, 0.0).astype(BF16)
    li = lax.broadcasted_iota(I32, (LANES, LANES), 0)
    lj = lax.broadcasted_iota(I32, (LANES, LANES), 1)
    upper_incl = jnp.where(li <= lj, 1.0, 0.0).astype(BF16)
    ones_ll = jnp.ones((LANES, LANES), BF16)

    def as_bf16(mask):
        return jnp.where(mask, 1.0, 0.0).astype(BF16)

    def row_total(x01):
        return _dot(x01, ones_ll)

    def expert_total(x01):
        return _dot(blockdiag, row_total(x01).astype(BF16))

    capf = float(cap)

    ag = ag_ref[0]

    def search(step, bounds):
        lo, hi = bounds
        mid = 0.5 * (lo + hi)
        ok = jnp.sum(jnp.where(ag >= mid, 1.0, 0.0), axis=1, keepdims=True) >= capf
        return jnp.where(ok, mid, lo), jnp.where(ok, hi, mid)

    lo_g, hi_g = lax.fori_loop(0, BISECT_STEPS, search,
                               (jnp.zeros((groups, 1), F32), jnp.full((groups, 1), 2.0, F32)))
    to_rows = as_bf16(lax.broadcasted_iota(I32, (rows, groups), 0) // chunks
                      == lax.broadcasted_iota(I32, (rows, groups), 1))
    lo = _dot_exact_rhs(to_rows, jnp.broadcast_to(lo_g, (groups, LANES)))
    hi = _dot_exact_rhs(to_rows, jnp.broadcast_to(hi_g, (groups, LANES)))

    gt = a >= hi
    eq = (a >= lo) & (a < hi)
    need = capf - expert_total(as_bf16(gt))
    eq01 = as_bf16(eq)
    eq_rank = (_dot(eq01, upper_incl) - eq01.astype(F32)
               + _dot(lower, row_total(eq01).astype(BF16)))
    sel = gt | (eq & (eq_rank < need))
    sel01 = as_bf16(sel)
    sel_rows = row_total(sel01)
    c_excl = _dot(lower, sel_rows.astype(BF16))
    c_incl = c_excl + sel_rows
    key = 2.0 * (_dot(sel01, upper_incl) + c_excl) - sel01.astype(F32)
    first_lane = as_bf16(lax.broadcasted_iota(I32, (8, LANES), 1) == 0)
    cx_s[...] = _dot_exact_rhs(first_lane, c_excl, _NT)
    ci_s[...] = _dot_exact_rhs(first_lane, c_incl, _NT)
    key_s[...] = key

    rb = min(rows, LANES)
    jcol = lax.broadcasted_iota(I32, (slots, rb), 0).astype(F32)
    rlane = lax.broadcasted_iota(I32, (slots, rb), 1)
    jl = lax.broadcasted_iota(I32, (slots, LANES), 0).astype(F32)
    lane8 = lax.broadcasted_iota(I32, (8, LANES), 1).astype(BF16)
    chunk8 = (lax.broadcasted_iota(I32, (8, rb), 1) % chunks).astype(BF16)
    ones8 = jnp.ones((8, LANES), BF16)

    def per_group(e, carry):
        r0 = pl.multiple_of(((e * chunks) // rb) * rb, rb)
        mine = ((rlane + r0) // chunks) == e
        c_excl_l = cx_s[0:1, pl.ds(r0, rb)]
        c_incl_l = ci_s[0:1, pl.ds(r0, rb)]
        onehot = as_bf16(mine & (c_excl_l <= jcol) & (jcol < c_incl_l))
        hit = _dot_exact_rhs(onehot, key_s[pl.ds(r0, rb), :]) == 2.0 * jl + 1.0
        hit01 = as_bf16(hit)
        tok = _dot(chunk8, onehot, _NT) * float(LANES) + _dot(lane8, hit01, _NT)
        gsel = jnp.where(hit, _dot_exact_rhs(onehot, a_ref[0, pl.ds(r0, rb), :]), 0.0)
        g = _dot_exact_rhs(ones8, gsel, _NT)
        idx_ref[0, pl.ds(e, 1), :] = tok[0:1, :cap].astype(I32)
        gate_ref[0, pl.ds(e, 1), :] = g[0:1, :cap]
        return carry

    lax.fori_loop(0, groups, per_group, 0, unroll=4 if slots == LANES else 2)


def _select(affT, cap):
    b, _, n = affT.shape
    chunks = n // LANES
    bstep = b if b * N_EXPERTS * chunks <= 512 else 1
    groups = bstep * N_EXPERTS
    rows = groups * chunks
    idx, gate = pl.pallas_call(
        functools.partial(_select_kernel, groups=groups, chunks=chunks, cap=cap),
        grid=(b // bstep,),
        in_specs=[pl.BlockSpec((1, groups, n), lambda bi: (bi, 0, 0)),
                  pl.BlockSpec((1, rows, LANES), lambda bi: (bi, 0, 0))],
        out_specs=[pl.BlockSpec((1, groups, cap), lambda bi: (bi, 0, 0))] * 2,
        out_shape=[jax.ShapeDtypeStruct((b // bstep, groups, cap), I32),
                   jax.ShapeDtypeStruct((b // bstep, groups, cap), F32)],
        scratch_shapes=[pltpu.VMEM((rows, LANES), F32), pltpu.VMEM((8, rows), F32), pltpu.VMEM((8, rows), F32)],
        compiler_params=_cparams(("arbitrary",)),
        name="select",
    )(affT.reshape(b // bstep, groups, n), affT.reshape(b // bstep, rows, LANES))
    return idx.reshape(b, N_EXPERTS, cap), gate.reshape(b, N_EXPERTS, cap)


def _dispatch_kernel(idx_ref, h_ref, xg_ref, buf, *, cap, eg):
    bi = pl.program_id(0)
    g0 = pl.program_id(1) * eg

    def per_expert(el, carry):
        base = ((bi * N_EXPERTS) + g0 + el) * cap

        def body(jo, c2):
            j0 = pl.multiple_of(jo * UNROLL, UNROLL)
            dst = buf.at[pl.ds(j0, UNROLL)]
            for u in range(UNROLL):
                t = idx_ref[base + j0 + u]
                dst[u:u + 1, :] = h_ref[0, pl.ds(t, 1), :]
            return c2

        lax.fori_loop(0, cap // UNROLL, body, 0)
        xg_ref[el, 0] = buf[...].astype(BF16)
        return carry

    lax.fori_loop(0, eg, per_expert, 0)


def _dispatch(idx, h2, cap, eg):
    b, n, _ = h2.shape
    grid_spec = pltpu.PrefetchScalarGridSpec(
        num_scalar_prefetch=1,
        grid=(b, N_EXPERTS // eg),
        in_specs=[pl.BlockSpec((1, n, D_MODEL), lambda bi, g, idx_r: (bi, 0, 0))],
        out_specs=pl.BlockSpec((eg, 1, cap, D_MODEL), lambda bi, g, idx_r: (g, bi, 0, 0)),
        scratch_shapes=[pltpu.VMEM((cap, D_MODEL), F32)],
    )
    return pl.pallas_call(
        functools.partial(_dispatch_kernel, cap=cap, eg=eg),
        grid_spec=grid_spec,
        out_shape=jax.ShapeDtypeStruct((N_EXPERTS, b, cap, D_MODEL), BF16),
        compiler_params=_cparams(("arbitrary", "arbitrary")),
        name="dispatch",
    )(idx.reshape(-1), h2)


def _combine_kernel(idx_ref, y_ref, acc_ref, *, cap, eg):
    bi = pl.program_id(0)
    g = pl.program_id(1)
    g0 = g * eg

    @pl.when(g == 0)
    def _():
        acc_ref[...] = jnp.zeros_like(acc_ref)

    def per_expert(el, carry):
        base = ((bi * N_EXPERTS) + g0 + el) * cap

        def body(jo, c2):
            j0 = pl.multiple_of(jo * UNROLL, UNROLL)
            toks = [idx_ref[base + j0 + u] for u in range(UNROLL)]
            olds = [acc_ref[0, pl.ds(t, 1), :] for t in toks]
            src = y_ref.at[el, 0, pl.ds(j0, UNROLL)]
            for u in range(UNROLL):
                acc_ref[0, pl.ds(toks[u], 1), :] = olds[u] + src[u:u + 1, :]
            return c2

        lax.fori_loop(0, cap // UNROLL, body, 0)
        return carry

    lax.fori_loop(0, eg, per_expert, 0)


def _combine(idx, y, n, cap, eg):
    b = y.shape[1]
    grid_spec = pltpu.PrefetchScalarGridSpec(
        num_scalar_prefetch=1,
        grid=(b, N_EXPERTS // eg),
        in_specs=[pl.BlockSpec((eg, 1, cap, D_MODEL), lambda bi, g, idx_r: (g, bi, 0, 0))],
        out_specs=pl.BlockSpec((1, n, D_MODEL), lambda bi, g, idx_r: (bi, 0, 0)),
    )
    return pl.pallas_call(
        functools.partial(_combine_kernel, cap=cap, eg=eg),
        grid_spec=grid_spec,
        out_shape=jax.ShapeDtypeStruct((b, n, D_MODEL), F32),
        compiler_params=_cparams(("arbitrary", "arbitrary")),
        name="combine",
    )(idx.reshape(-1), y)


def _ffn_kernel(xg_ref, gate_ref, wg_ref, wu_ref, wd_ref, y_ref, wg_s, wu_s, wd_s):
    @pl.when(pl.program_id(1) == 0)
    def _():
        wg_s[...] = wg_ref[0, 0].astype(BF16)
        wu_s[...] = wu_ref[0, 0].astype(BF16)
        wd_s[...] = wd_ref[0, 0].astype(BF16)

    rows = xg_ref.shape[1]
    for r0 in range(0, rows, FFN_SUB):
        rsl = slice(r0, r0 + FFN_SUB)
        x = xg_ref[0, rsl, :]
        a = _dot(x, wg_s[...])
        u = _dot(x, wu_s[...])
        hmid = (a * (1.0 / (1.0 + jnp.exp(-a))) * u).astype(BF16)
        y_ref[0, rsl, :] = _dot(hmid, wd_s[...]) * gate_ref[0, rsl, :]


def _ffn(xg, gate, w_gate, w_up, w_down, layer):
    _, r, _ = xg.shape
    tm = min(FFN_TM, r)
    w_spec = pl.BlockSpec((1, 1, D_MODEL, D_MODEL), lambda e, m: (layer, e, 0, 0))
    return pl.pallas_call(
        _ffn_kernel,
        grid=(N_EXPERTS, r // tm),
        in_specs=[
            pl.BlockSpec((1, tm, D_MODEL), lambda e, m: (e, m, 0)),
            pl.BlockSpec((1, tm, 1), lambda e, m: (e, m, 0)),
            w_spec, w_spec, w_spec,
        ],
        out_specs=pl.BlockSpec((1, tm, D_MODEL), lambda e, m: (e, m, 0)),
        out_shape=jax.ShapeDtypeStruct((N_EXPERTS, r, D_MODEL), F32),
        scratch_shapes=[pltpu.VMEM((D_MODEL, D_MODEL), BF16)] * 3,
        compiler_params=_cparams(("arbitrary", "arbitrary")),
        name="expert_ffn",
    )(xg, gate, w_gate, w_up, w_down)


def _moe(h2, affT, w_gate, w_up, w_down, layer, eg):
    b, n, _ = h2.shape
    cap = (CAPACITY_FACTOR * n) // N_EXPERTS
    idx, gate = _select(affT, cap)
    xg = _dispatch(idx, h2, cap, eg)
    gate_rows = jnp.swapaxes(gate, 0, 1).reshape(N_EXPERTS, b * cap, 1)
    y = _ffn(xg.reshape(N_EXPERTS, b * cap, D_MODEL), gate_rows, w_gate, w_up, w_down, layer)
    return _combine(idx, y.reshape(N_EXPERTS, b, cap, D_MODEL), n, cap, eg)


def _residual_kernel(x_ref, acc_ref, mod_ref, out_ref):
    out_ref[0] = x_ref[0] + mod_ref[0, 5:6, :] * acc_ref[0]


def _residual(x, acc, mod, mod_row):
    b, n, _ = x.shape
    tok_spec = pl.BlockSpec((1, TOK, D_MODEL), lambda bi, t: (bi, t, 0))
    return pl.pallas_call(
        _residual_kernel,
        grid=(b, n // TOK),
        in_specs=[tok_spec, tok_spec, pl.BlockSpec((1, 6, D_MODEL), lambda bi, t: (mod_row(bi), 0, 0))],
        out_specs=tok_spec,
        out_shape=jax.ShapeDtypeStruct((b, n, D_MODEL), F32),
        compiler_params=_cparams(("arbitrary", "arbitrary")),
        name="ffn_residual",
    )(x, acc, mod)


def _rope_tables(n):
    t = jnp.arange(n)
    inv_freq = ROPE_THETA ** (-jnp.arange(AXIS_PAIRS, dtype=F32) / AXIS_PAIRS)
    ang_r = (t // GRID_W).astype(F32)[None, :] * inv_freq[:, None]
    ang_c = (t % GRID_W).astype(F32)[None, :] * inv_freq[:, None]
    ang = jnp.concatenate([ang_r, ang_c], axis=0)
    return jnp.cos(ang), jnp.sin(ang)


def _band_chunk_starts(nt):
    return (0, None, nt - 3)


def _window_bias(nt):
    u = jnp.arange(TOK)[:, None]
    q = jnp.arange(TOK)[None, :]
    variants = []
    for rel0 in (0, -1, -2):
        slots = []
        for slot in range(3):
            d = (rel0 + slot) * TOK + u - q
            slots.append(jnp.where(jnp.abs(d) <= WINDOW, 0.0, NEG_INF).astype(F32))
        variants.append(jnp.stack(slots))
    return jnp.stack(variants)[:, None]


def _neighborhood_bias(rpb, rows):
    nt = rows // ROWS_PER_TOK
    kc = jnp.arange(GRID_W)[:, None]
    qc = jnp.arange(GRID_W)[None, :]
    cs = jnp.clip(qc - KW // 2, 0, GRID_W - KW)
    col_ok = (kc >= cs) & (kc < cs + KW)
    diag = (kc - qc + (KW - 1))[None] == jnp.arange(2 * KW - 1)[:, None, None]
    toep = jnp.einsum("hdj,jkq->hdkq", rpb.astype(F32) * LOG2E, diag.astype(F32),
                      precision=lax.Precision.HIGHEST)
    toep = jnp.where(col_ok, toep, NEG_INF)
    masked = jnp.full((rpb.shape[0], GRID_W, GRID_W), NEG_INF, F32)
    variants = []
    for j, chunk0 in ((0, 0), (1, 0), (nt - 1, nt - 3)):
        slots = []
        for slot in range(3):
            block_rows = []
            for a in range(ROWS_PER_TOK):
                kr = (chunk0 + slot) * ROWS_PER_TOK + a
                blocks = []
                for b in range(ROWS_PER_TOK):
                    qr = j * ROWS_PER_TOK + b
                    rs = min(max(qr - KH // 2, 0), rows - KH)
                    ok = rs <= kr < rs + KH
                    blocks.append(toep[:, kr - qr + (KH - 1)] if ok else masked)
                block_rows.append(jnp.concatenate(blocks, axis=-1))
            slots.append(jnp.concatenate(block_rows, axis=-2))
        variants.append(jnp.stack(slots, axis=1))
    return jnp.stack(variants)


def kernel(x_prompt, x_sample, cache_k, cache_v, c, c_ctx, w_ada, b_ada, norm_attn_g, norm_ffn_g,
           w_qkv, q_norm_g, k_norm_g, w_o, sink_logits, rel_pos_bias, w_router, w_gate, w_up, w_down):
    bp, n_p, _ = x_prompt.shape
    bs, n_s, _ = x_sample.shape
    past = cache_k.shape[2]
    assert n_p % TOK == 0 and n_s % TOK == 0 and past % TOK == 0
    assert n_s // TOK >= 3 and 1 + bs <= COND_ROWS

    cond = jnp.zeros((COND_ROWS, D_MODEL), F32).at[0].set(c_ctx).at[1:1 + bs].set(c)
    mods = _ada_modulation(cond, w_ada, b_ada).reshape(DEPTH, COND_ROWS, 6, D_MODEL)
    row_p = lambda bi: 0
    row_s = lambda bi: bi + 1

    nc = past // TOK
    kc_all = jnp.transpose(cache_k, (1, 0, 3, 2, 4)).astype(BF16)
    kc_all = kc_all.reshape(DEPTH, bs, N_KV_HEADS, nc, TOK, HEAD_DIM)
    vc_all = jnp.transpose(cache_v, (1, 0, 3, 4, 2)).astype(BF16)
    vc_all = jnp.transpose(vc_all.reshape(DEPTH, bs, N_KV_HEADS, HEAD_DIM, nc, TOK), (0, 1, 2, 4, 3, 5))
    vc_all = jnp.concatenate(
        [vc_all, jnp.ones((DEPTH, bs, N_KV_HEADS, nc, VT_ROWS - HEAD_DIM, TOK), BF16)], axis=4)

    rope_tabs = _rope_tables(n_s)
    win_bias = _window_bias(n_s // TOK)

    xp, xs = x_prompt, x_sample
    acc_p = acc_s = None
    new_k, new_v = [], []
    for i in range(DEPTH):
        kind = i % 3
        sink = sink_logits[i // 3] if kind == 1 else None
        prev = mods[i - 1] if i > 0 else None

        xp, qT, k, vT, ck, cv = _pre_attn(xp, acc_p, prev, mods[i], row_p, norm_attn_g[i], w_qkv, i,
                                          q_norm_g[i], k_norm_g[i], None, True)
        new_k.append(ck.reshape(bp, n_p, N_KV_HEADS, HEAD_DIM))
        new_v.append(cv.reshape(bp, n_p, N_KV_HEADS, HEAD_DIM))
        oT = _attention(qT, k, vT, None, sink, None)
        xp, h2, affT = _post_attn(oT, xp, mods[i], row_p, norm_ffn_g[i], w_o, i, w_router[i], True)
        acc_p = _moe(h2, affT, w_gate, w_up, w_down, i, N_EXPERTS)

        tabs = rope_tabs if kind != 2 else None
        xs, qT, k, vT = _pre_attn(xs, acc_s, prev, mods[i], row_s, norm_attn_g[i], w_qkv, i,
                                  q_norm_g[i], k_norm_g[i], tabs, False)
        ctx = (kc_all[i], vc_all[i])
        if kind == 0:
            oT = _attention(qT, k, vT, ctx, None, None)
        elif kind == 1:
            oT = _attention(qT, k, vT, ctx, sink, win_bias)
        else:
            oT = _attention(qT, k, vT, ctx, None, _neighborhood_bias(rel_pos_bias[i // 3], n_s // GRID_W))
        xs, h2, affT = _post_attn(oT, xs, mods[i], row_s, norm_ffn_g[i], w_o, i, w_router[i], False)
        acc_s = _moe(h2, affT, w_gate, w_up, w_down, i, 1)

    y_prompt = _residual(xp, acc_p, mods[DEPTH - 1], row_p)
    y_sample = _residual(xs, acc_s, mods[DEPTH - 1], row_s)
    return (y_prompt, y_sample, jnp.stack(new_k, axis=1), jnp.stack(new_v, axis=1))
```

```python
import functools

import jax
import jax.numpy as jnp
from jax import lax
from jax.experimental import pallas as pl
from jax.experimental.pallas import tpu as pltpu

F32 = jnp.float32
BF16 = jnp.bfloat16
I32 = jnp.int32

D_MODEL = 1024
DEPTH = 4
GRID_W = 64
N_HEADS = 16
N_KV_HEADS = 4
GROUP = N_HEADS // N_KV_HEADS
HEAD_DIM = 64
AXIS_PAIRS = HEAD_DIM // 4
ROPE_THETA = 10000.0
QKV_DIM = (N_HEADS + 2 * N_KV_HEADS) * HEAD_DIM
Q_DIM = N_HEADS * HEAD_DIM
KV_DIM = N_KV_HEADS * HEAD_DIM
WINDOW = 128
KH = 8
KW = 16
N_EXPERTS = 16
CAPACITY_FACTOR = 2
SCALE = HEAD_DIM ** -0.5
LOG2E = 1.4426950408889634
NEG_INF = -1e30
EPS = 1e-6

LANES = 128
TOK = 256
VT_ROWS = 64 + 16
AHEAD = 2
N_SBUF = 4
SUBTILES = 4
ROWS_PER_TOK = TOK // GRID_W
COND_ROWS = 8
FFN_TM = 512
FFN_SUB = 256
UNROLL = 8
BISECT_STEPS = 64
VMEM_LIMIT = 56 * 1024 * 1024


def _cparams(sem):
    return pltpu.CompilerParams(dimension_semantics=sem, vmem_limit_bytes=VMEM_LIMIT)


def _split3(x):
    hi = x.astype(BF16)
    r = x - hi.astype(F32)
    mid = r.astype(BF16)
    lo = (r - mid.astype(F32)).astype(BF16)
    return hi, mid, lo


def _dot(a, b, dims=(((1,), (0,)), ((), ()))):
    return lax.dot_general(a, b, dims, preferred_element_type=F32)


_NT = (((1,), (1,)), ((), ()))
_TN = (((0,), (0,)), ((), ()))


def _dot_exact_rhs(onehot_bf16, x_f32, dims=(((1,), (0,)), ((), ()))):
    hi, mid, lo = _split3(x_f32)
    return _dot(onehot_bf16, hi, dims) + _dot(onehot_bf16, mid, dims) + _dot(onehot_bf16, lo, dims)


def _dot_f32(a, b, dims):
    a_hi, a_mid, _ = _split3(a)
    b_hi, b_mid, _ = _split3(b)
    return (_dot(a_hi, b_hi, dims) + _dot(a_hi, b_mid, dims) + _dot(a_mid, b_hi, dims))


def _mod_rmsnorm(x, g, shift, scale):
    ms = jnp.mean(x * x, axis=-1, keepdims=True)
    return (x * lax.rsqrt(ms + EPS) * g) * (1.0 + scale) + shift


ADA_TN = 1536


def _ada_kernel(cond_ref, w_ref, b_ref, out_ref):
    cnd = cond_ref[...]
    act = cnd * (1.0 / (1.0 + jnp.exp(-cnd)))
    out_ref[0] = _dot_f32(act, w_ref[0], (((1,), (0,)), ((), ()))) + b_ref[0]


def _ada_modulation(cond, w_ada, b_ada):
    n_out = 6 * D_MODEL
    return pl.pallas_call(
        _ada_kernel,
        grid=(DEPTH, n_out // ADA_TN),
        in_specs=[
            pl.BlockSpec((COND_ROWS, D_MODEL), lambda i, n: (0, 0)),
            pl.BlockSpec((1, D_MODEL, ADA_TN), lambda i, n: (i, 0, n)),
            pl.BlockSpec((1, 1, ADA_TN), lambda i, n: (i, 0, n)),
        ],
        out_specs=pl.BlockSpec((1, COND_ROWS, ADA_TN), lambda i, n: (i, 0, n)),
        out_shape=jax.ShapeDtypeStruct((DEPTH, COND_ROWS, n_out), F32),
        compiler_params=_cparams(("arbitrary", "arbitrary")),
        name="ada_modulation",
    )(cond, w_ada, b_ada.reshape(DEPTH, 1, n_out))


def _pre_attn_kernel(*refs, fuse_residual, rope, emit_cache, subtiles, batch_tiles):
    it = iter(refs)
    x_ref = next(it)
    if fuse_residual:
        acc_ref = next(it)
        pmod_ref = next(it)
    mod_ref = next(it)
    g_ref = next(it)
    w_ref = next(it)
    gq_ref = next(it)
    gk_ref = next(it)
    if rope:
        cos_ref = next(it)
        sin_ref = next(it)
    if fuse_residual:
        xo_ref = next(it)
    qT_ref = next(it)
    k_ref = next(it)
    vT_ref = next(it)
    if emit_cache:
        ck_ref = next(it)
        cv_ref = next(it)
    w_s = next(it)

    @pl.when((pl.program_id(0) == 0) & (pl.program_id(1) == 0))
    def _():
        w_s[...] = w_ref[0].astype(BF16)

    def head_norm(blk, gain):
        ss = jnp.sum(blk * blk, axis=0, keepdims=True)
        return blk * lax.rsqrt(ss * (1.0 / HEAD_DIM) + EPS) * gain

    def rotate(blk, tsl):
        if not rope:
            return blk
        cs = cos_ref[:, tsl]
        sn = sin_ref[:, tsl]
        parts = []
        for a in range(2):
            x1 = blk[32 * a:32 * a + 16]
            x2 = blk[32 * a + 16:32 * a + 32]
            c = cs[16 * a:16 * a + 16]
            s = sn[16 * a:16 * a + 16]
            parts.append(x1 * c - x2 * s)
            parts.append(x2 * c + x1 * s)
        return jnp.concatenate(parts, axis=0)

    gq = gq_ref[...]
    gk = gk_ref[...]
    for bt, st in [(bt, st) for bt in range(batch_tiles) for st in range(subtiles)]:
        tsl = slice(st * TOK, (st + 1) * TOK)
        x = x_ref[bt, tsl, :]
        if fuse_residual:
            x = x + pmod_ref[0, 5:6, :] * acc_ref[bt, tsl, :]
            xo_ref[bt, tsl, :] = x
        h = _mod_rmsnorm(x, g_ref[...], mod_ref[0, 0:1, :], mod_ref[0, 1:2, :])
        qkv = _dot(h.astype(BF16), w_s[...])
        if emit_cache:
            cv_ref[bt, tsl, :] = qkv[:, Q_DIM + KV_DIM:]
        qkvT = qkv.T
        for hd in range(N_HEADS):
            blk = qkvT[HEAD_DIM * hd:HEAD_DIM * (hd + 1)]
            qT_ref[bt, hd, :, tsl] = (rotate(head_norm(blk, gq), tsl) * (SCALE * LOG2E)).astype(BF16)
        kn_all = []
        for kv in range(N_KV_HEADS):
            off = Q_DIM + HEAD_DIM * kv
            kn_all.append(rotate(head_norm(qkvT[off:off + HEAD_DIM], gk), tsl))
            offv = Q_DIM + KV_DIM + HEAD_DIM * kv
            vT_ref[bt, kv, st, :HEAD_DIM] = qkvT[offv:offv + HEAD_DIM].astype(BF16)
            vT_ref[bt, kv, st, HEAD_DIM:] = jnp.ones((VT_ROWS - HEAD_DIM, TOK), BF16)
        kn_tok = jnp.concatenate(kn_all, axis=0).T
        for kv in range(N_KV_HEADS):
            k_ref[bt, kv, st] = kn_tok[:, HEAD_DIM * kv:HEAD_DIM * (kv + 1)].astype(BF16)
        if emit_cache:
            ck_ref[bt, tsl, :] = kn_tok


def _step_tiles(b, nt, shared_modulation):
    if nt % SUBTILES == 0:
        return SUBTILES, 1
    if shared_modulation and b % SUBTILES == 0:
        return 1, SUBTILES
    return 1, 1


def _pre_attn(x, acc, prev_mod, mod, mod_row, g, w_qkv, layer, gq, gk, rope_tabs, emit_cache):
    b, n, _ = x.shape
    nt = n // TOK
    sub, bsub = _step_tiles(b, nt, emit_cache)
    fuse = acc is not None
    rope = rope_tabs is not None
    tok_spec = pl.BlockSpec((bsub, sub * TOK, D_MODEL), lambda bi, t: (bi, t, 0))
    mod_spec = pl.BlockSpec((1, 6, D_MODEL), lambda bi, t: (mod_row(bi), 0, 0))
    ins, in_specs = [x], [tok_spec]
    if fuse:
        ins += [acc, prev_mod]
        in_specs += [tok_spec, mod_spec]
    ins += [mod, g.reshape(1, D_MODEL), w_qkv,
            jnp.broadcast_to(gq[:, None], (HEAD_DIM, TOK)),
            jnp.broadcast_to(gk[:, None], (HEAD_DIM, TOK))]
    in_specs += [
        mod_spec,
        pl.BlockSpec((1, D_MODEL), lambda bi, t: (0, 0)),
        pl.BlockSpec((1, D_MODEL, QKV_DIM), lambda bi, t: (layer, 0, 0)),
        pl.BlockSpec((HEAD_DIM, TOK), lambda bi, t: (0, 0)),
        pl.BlockSpec((HEAD_DIM, TOK), lambda bi, t: (0, 0)),
    ]
    if rope:
        ins += list(rope_tabs)
        in_specs += [pl.BlockSpec((2 * AXIS_PAIRS, sub * TOK), lambda bi, t: (0, t))] * 2
    out_shape, out_specs = [], []
    if fuse:
        out_shape.append(jax.ShapeDtypeStruct((b, n, D_MODEL), F32))
        out_specs.append(tok_spec)
    out_shape += [
        jax.ShapeDtypeStruct((b, N_HEADS, HEAD_DIM, n), BF16),
        jax.ShapeDtypeStruct((b, N_KV_HEADS, nt, TOK, HEAD_DIM), BF16),
        jax.ShapeDtypeStruct((b, N_KV_HEADS, nt, VT_ROWS, TOK), BF16),
    ]
    out_specs += [
        pl.BlockSpec((bsub, N_HEADS, HEAD_DIM, sub * TOK), lambda bi, t: (bi, 0, 0, t)),
        pl.BlockSpec((bsub, N_KV_HEADS, sub, TOK, HEAD_DIM), lambda bi, t: (bi, 0, t, 0, 0)),
        pl.BlockSpec((bsub, N_KV_HEADS, sub, VT_ROWS, TOK), lambda bi, t: (bi, 0, t, 0, 0)),
    ]
    if emit_cache:
        out_shape += [jax.ShapeDtypeStruct((b, n, KV_DIM), F32)] * 2
        out_specs += [pl.BlockSpec((bsub, sub * TOK, KV_DIM), lambda bi, t: (bi, t, 0))] * 2
    outs = pl.pallas_call(
        functools.partial(_pre_attn_kernel, fuse_residual=fuse, rope=rope, emit_cache=emit_cache,
                          subtiles=sub, batch_tiles=bsub),
        grid=(b // bsub, nt // sub),
        in_specs=in_specs,
        out_specs=out_specs,
        out_shape=out_shape,
        scratch_shapes=[pltpu.VMEM((D_MODEL, QKV_DIM), BF16)],
        compiler_params=_cparams(("arbitrary", "arbitrary")),
        name="pre_attn",
    )(*ins)
    outs = list(outs)
    x_new = outs.pop(0) if fuse else x
    return [x_new] + outs


def _attn_kernel(*refs, mode, n_local, has_ctx, has_sink, bias_heads):
    it = iter(refs)
    qT_ref = next(it)
    if mode == "dense":
        k_ref = next(it)
        vT_ref = next(it)
    else:
        k_refs = [next(it) for _ in range(3)]
        vT_refs = [next(it) for _ in range(3)]
        bias_ref = next(it)
    if has_ctx:
        kc_ref = next(it)
        vcT_ref = next(it)
    if has_sink:
        sink_ref = next(it)
    oT_ref = next(it)
    m_s = next(it)
    acc_s = next(it)
    s_buf = next(it)

    kv = pl.program_id(1)
    for hd in range(GROUP):
        acc_s[hd, :HEAD_DIM] = jnp.zeros((HEAD_DIM, TOK), F32)
        if has_sink:
            m_s[hd] = jnp.full((1, TOK), sink_ref[kv * GROUP + hd] * LOG2E, F32)
            acc_s[hd, HEAD_DIM:] = jnp.ones((VT_ROWS - HEAD_DIM, TOK), F32)
        else:
            m_s[hd] = jnp.full((1, TOK), NEG_INF, F32)
            acc_s[hd, HEAD_DIM:] = jnp.zeros((VT_ROWS - HEAD_DIM, TOK), F32)

    def scores(par, kch):
        for hd in range(GROUP):
            s_buf[par, hd] = _dot(kch, qT_ref[0, hd])

    def consume(par, vch, bias_of_head):
        for hd in range(GROUP):
            s = s_buf[par, hd]
            if bias_of_head is not None:
                s = s + bias_of_head(hd)
            m_prev = m_s[hd]
            m_new = jnp.maximum(m_prev, jnp.max(s, axis=0, keepdims=True))
            alpha = jnp.exp2(m_prev - m_new)
            p = jnp.exp2(s - m_new)
            acc_s[hd] = alpha * acc_s[hd] + _dot(vch, p.astype(BF16))
            m_s[hd] = m_new

    static_chunks = []
    if mode == "band":
        for slot in range(3):
            if bias_heads == 1:
                bias_fn = lambda hd, slot=slot: bias_ref[0, 0, slot]
            else:
                bias_fn = lambda hd, slot=slot: bias_ref[0, hd, slot]
            static_chunks.append((k_refs[slot].at[0, 0, 0], vT_refs[slot].at[0, 0, 0], bias_fn))
    if has_ctx:
        for c in range(kc_ref.shape[2]):
            static_chunks.append((kc_ref.at[0, 0, c], vcT_ref.at[0, 0, c], None))

    n_loop = 0
    if mode == "dense":
        n_loop = max(0, (n_local - AHEAD) // N_SBUF) * N_SBUF
        static_chunks = ([(k_ref.at[0, 0, c], vT_ref.at[0, 0, c], None) for c in range(n_loop, n_local)]
                         + static_chunks)
    for c in range(AHEAD):
        if c < n_loop:
            scores(c % N_SBUF, k_ref[0, 0, c])
        elif c - n_loop < len(static_chunks):
            scores(c % N_SBUF, static_chunks[c - n_loop][0][...])
    if n_loop:
        def body(i, carry):
            c0 = N_SBUF * i
            for u in range(N_SBUF):
                scores((u + AHEAD) % N_SBUF, k_ref[0, 0, c0 + u + AHEAD])
                consume(u, vT_ref[0, 0, c0 + u], None)
            return carry
        lax.fori_loop(0, n_loop // N_SBUF, body, 0)
    for i, (_, v_at, bias_fn) in enumerate(static_chunks):
        if i + AHEAD < len(static_chunks):
            scores((n_loop + i + AHEAD) % N_SBUF, static_chunks[i + AHEAD][0][...])
        consume((n_loop + i) % N_SBUF, v_at[...], bias_fn)

    for hd in range(GROUP):
        oT_ref[0, hd] = (acc_s[hd, :HEAD_DIM] / acc_s[hd, HEAD_DIM:HEAD_DIM + 1]).astype(BF16)


def _attention(qT, k, vT, ctx, sink, bias):
    b, _, _, n = qT.shape
    nt = n // TOK
    mode = "dense" if bias is None else "band"
    ins = [qT]
    in_specs = [pl.BlockSpec((1, GROUP, HEAD_DIM, TOK), lambda bi, kv, j: (bi, kv, 0, j))]
    if mode == "dense":
        ins += [k, vT]
        in_specs += [
            pl.BlockSpec((1, 1, nt, TOK, HEAD_DIM), lambda bi, kv, j: (bi, kv, 0, 0, 0)),
            pl.BlockSpec((1, 1, nt, VT_ROWS, TOK), lambda bi, kv, j: (bi, kv, 0, 0, 0)),
        ]
        bias_heads = 0
    else:
        def base(j):
            return jnp.clip(j - 1, 0, nt - 3)
        ins += [k, k, k, vT, vT, vT, bias]
        for slot in range(3):
            in_specs.append(pl.BlockSpec(
                (1, 1, 1, TOK, HEAD_DIM),
                lambda bi, kv, j, slot=slot: (bi, kv, base(j) + slot, 0, 0)))
        for slot in range(3):
            in_specs.append(pl.BlockSpec(
                (1, 1, 1, VT_ROWS, TOK),
                lambda bi, kv, j, slot=slot: (bi, kv, base(j) + slot, 0, 0)))
        bias_heads = 1 if bias.shape[1] == 1 else GROUP

        def variant(j):
            return (j > 0).astype(I32) + (j == nt - 1).astype(I32)
        if bias_heads == 1:
            in_specs.append(pl.BlockSpec((1, 1, 3, TOK, TOK), lambda bi, kv, j: (variant(j), 0, 0, 0, 0)))
        else:
            in_specs.append(pl.BlockSpec((1, GROUP, 3, TOK, TOK), lambda bi, kv, j: (variant(j), kv, 0, 0, 0)))
    if ctx is not None:
        kc, vcT = ctx
        nc = kc.shape[2]
        ins += [kc, vcT]
        in_specs += [
            pl.BlockSpec((1, 1, nc, TOK, HEAD_DIM), lambda bi, kv, j: (bi, kv, 0, 0, 0)),
            pl.BlockSpec((1, 1, nc, VT_ROWS, TOK), lambda bi, kv, j: (bi, kv, 0, 0, 0)),
        ]
    if sink is not None:
        ins.append(sink)
        in_specs.append(pl.BlockSpec(memory_space=pltpu.SMEM))
    return pl.pallas_call(
        functools.partial(_attn_kernel, mode=mode, n_local=nt, has_ctx=ctx is not None,
                          has_sink=sink is not None, bias_heads=bias_heads),
        grid=(b, N_KV_HEADS, nt),
        in_specs=in_specs,
        out_specs=pl.BlockSpec((1, GROUP, HEAD_DIM, TOK), lambda bi, kv, j: (bi, kv, 0, j)),
        out_shape=jax.ShapeDtypeStruct((b, N_HEADS, HEAD_DIM, n), BF16),
        scratch_shapes=[
            pltpu.VMEM((GROUP, 1, TOK), F32),
            pltpu.VMEM((GROUP, VT_ROWS, TOK), F32),
            pltpu.VMEM((N_SBUF, GROUP, TOK, TOK), F32),
        ],
        compiler_params=_cparams(("arbitrary", "arbitrary", "arbitrary")),
        name="attention_" + mode,
    )(*ins)


def _post_attn_kernel(oT_ref, x_ref, mod_ref, g_ref, wo_ref, wrT_ref, xm_ref, h2_ref, affT_ref, wo_s,
                      *, subtiles, batch_tiles):
    @pl.when((pl.program_id(0) == 0) & (pl.program_id(1) == 0))
    def _():
        wo_s[...] = wo_ref[0].astype(BF16)

    for bt, st in [(bt, st) for bt in range(batch_tiles) for st in range(subtiles)]:
        tsl = slice(st * TOK, (st + 1) * TOK)
        oT = oT_ref[bt, :, :, tsl].reshape(Q_DIM, TOK)
        y = _dot(oT, wo_s[...], _TN)
        xm = x_ref[bt, tsl, :] + mod_ref[0, 2:3, :] * y
        xm_ref[bt, tsl, :] = xm
        h2 = _mod_rmsnorm(xm, g_ref[...], mod_ref[0, 3:4, :], mod_ref[0, 4:5, :])
        h2_ref[bt, tsl, :] = h2
        logitsT = _dot_f32(wrT_ref[...], h2, _NT)
        mx = jnp.max(logitsT, axis=0, keepdims=True)
        e = jnp.exp(logitsT - mx)
        affT_ref[bt, :, tsl] = e / jnp.sum(e, axis=0, keepdims=True)


def _post_attn(oT, x, mod, mod_row, g, w_o, layer, w_router, shared_modulation):
    b, n, _ = x.shape
    nt = n // TOK
    sub, bsub = _step_tiles(b, nt, shared_modulation)
    tok_spec = pl.BlockSpec((bsub, sub * TOK, D_MODEL), lambda bi, t: (bi, t, 0))
    return pl.pallas_call(
        functools.partial(_post_attn_kernel, subtiles=sub, batch_tiles=bsub),
        grid=(b // bsub, nt // sub),
        in_specs=[
            pl.BlockSpec((bsub, N_HEADS, HEAD_DIM, sub * TOK), lambda bi, t: (bi, 0, 0, t)),
            tok_spec,
            pl.BlockSpec((1, 6, D_MODEL), lambda bi, t: (mod_row(bi), 0, 0)),
            pl.BlockSpec((1, D_MODEL), lambda bi, t: (0, 0)),
            pl.BlockSpec((1, Q_DIM, D_MODEL), lambda bi, t: (layer, 0, 0)),
            pl.BlockSpec((N_EXPERTS, D_MODEL), lambda bi, t: (0, 0)),
        ],
        out_specs=[tok_spec, tok_spec, pl.BlockSpec((bsub, N_EXPERTS, sub * TOK), lambda bi, t: (bi, 0, t))],
        out_shape=[
            jax.ShapeDtypeStruct((b, n, D_MODEL), F32),
            jax.ShapeDtypeStruct((b, n, D_MODEL), F32),
            jax.ShapeDtypeStruct((b, N_EXPERTS, n), F32),
        ],
        scratch_shapes=[pltpu.VMEM((Q_DIM, D_MODEL), BF16)],
        compiler_params=_cparams(("arbitrary", "arbitrary")),
        name="post_attn",
    )(oT, x, mod, g.reshape(1, D_MODEL), w_o, w_router.T)


def _select_kernel(ag_ref, a_ref, idx_ref, gate_ref, key_s, cx_s, ci_s, *, groups, chunks, cap):
    rows = groups * chunks
    slots = max(cap, LANES)
    a = a_ref[0]

    ri = lax.broadcasted_iota(I32, (rows, rows), 0)
    ci = lax.broadcasted_iota(I32, (rows, rows), 1)
    same = (ri // chunks) == (ci // chunks)
    blockdiag = jnp.where(same, 1.0, 0.0).astype(BF16)
    lower = jnp.where(same & (ci < ri), 1.0, 0.0).astype(BF16)
    li = lax.broadcasted_iota(I32, (LANES, LANES), 0)
    lj = lax.broadcasted_iota(I32, (LANES, LANES), 1)
    upper_incl = jnp.where(li <= lj, 1.0, 0.0).astype(BF16)
    ones_ll = jnp.ones((LANES, LANES), BF16)

    def as_bf16(mask):
        return jnp.where(mask, 1.0, 0.0).astype(BF16)

    def row_total(x01):
        return _dot(x01, ones_ll)

    def expert_total(x01):
        return _dot(blockdiag, row_total(x01).astype(BF16))

    capf = float(cap)

    ag = ag_ref[0]

    def search(step, bounds):
        lo, hi = bounds
        mid = 0.5 * (lo + hi)
        ok = jnp.sum(jnp.where(ag >= mid, 1.0, 0.0), axis=1, keepdims=True) >= capf
        return jnp.where(ok, mid, lo), jnp.where(ok, hi, mid)

    lo_g, hi_g = lax.fori_loop(0, BISECT_STEPS, search,
                               (jnp.zeros((groups, 1), F32), jnp.full((groups, 1), 2.0, F32)))
    to_rows = as_bf16(lax.broadcasted_iota(I32, (rows, groups), 0) // chunks
                      == lax.broadcasted_iota(I32, (rows, groups), 1))
    lo = _dot_exact_rhs(to_rows, jnp.broadcast_to(lo_g, (groups, LANES)))
    hi = _dot_exact_rhs(to_rows, jnp.broadcast_to(hi_g, (groups, LANES)))

    gt = a >= hi
    eq = (a >= lo) & (a < hi)
    need = capf - expert_total(as_bf16(gt))
    eq01 = as_bf16(eq)
    eq_rank = (_dot(eq01, upper_incl) - eq01.astype(F32)
               + _dot(lower, row_total(eq01).astype(BF16)))
    sel = gt | (eq & (eq_rank < need))
    sel01 = as_bf16(sel)
    sel_rows = row_total(sel01)
    c_excl = _dot(lower, sel_rows.astype(BF16))
    c_incl = c_excl + sel_rows
    key = 2.0 * (_dot(sel01, upper_incl) + c_excl) - sel01.astype(F32)
    first_lane = as_bf16(lax.broadcasted_iota(I32, (8, LANES), 1) == 0)
    cx_s[...] = _dot_exact_rhs(first_lane, c_excl, _NT)
    ci_s[...] = _dot_exact_rhs(first_lane, c_incl, _NT)
    key_s[...] = key

    rb = min(rows, LANES)
    jcol = lax.broadcasted_iota(I32, (slots, rb), 0).astype(F32)
    rlane = lax.broadcasted_iota(I32, (slots, rb), 1)
    jl = lax.broadcasted_iota(I32, (slots, LANES), 0).astype(F32)
    lane8 = lax.broadcasted_iota(I32, (8, LANES), 1).astype(BF16)
    chunk8 = (lax.broadcasted_iota(I32, (8, rb), 1) % chunks).astype(BF16)
    ones8 = jnp.ones((8, LANES), BF16)

    def per_group(e, carry):
        r0 = pl.multiple_of(((e * chunks) // rb) * rb, rb)
        mine = ((rlane + r0) // chunks) == e
        c_excl_l = cx_s[0:1, pl.ds(r0, rb)]
        c_incl_l = ci_s[0:1, pl.ds(r0, rb)]
        onehot = as_bf16(mine & (c_excl_l <= jcol) & (jcol < c_incl_l))
        hit = _dot_exact_rhs(onehot, key_s[pl.ds(r0, rb), :]) == 2.0 * jl + 1.0
        hit01 = as_bf16(hit)
        tok = _dot(chunk8, onehot, _NT) * float(LANES) + _dot(lane8, hit01, _NT)
        gsel = jnp.where(hit, _dot_exact_rhs(onehot, a_ref[0, pl.ds(r0, rb), :]), 0.0)
        g = _dot_exact_rhs(ones8, gsel, _NT)
        idx_ref[0, pl.ds(e, 1), :] = tok[0:1, :cap].astype(I32)
        gate_ref[0, pl.ds(e, 1), :] = g[0:1, :cap]
        return carry

    lax.fori_loop(0, groups, per_group, 0, unroll=4 if slots == LANES else 2)


def _select(affT, cap):
    b, _, n = affT.shape
    chunks = n // LANES
    bstep = b if b * N_EXPERTS * chunks <= 512 else 1
    groups = bstep * N_EXPERTS
    rows = groups * chunks
    idx, gate = pl.pallas_call(
        functools.partial(_select_kernel, groups=groups, chunks=chunks, cap=cap),
        grid=(b // bstep,),
        in_specs=[pl.BlockSpec((1, groups, n), lambda bi: (bi, 0, 0)),
                  pl.BlockSpec((1, rows, LANES), lambda bi: (bi, 0, 0))],
        out_specs=[pl.BlockSpec((1, groups, cap), lambda bi: (bi, 0, 0))] * 2,
        out_shape=[jax.ShapeDtypeStruct((b // bstep, groups, cap), I32),
                   jax.ShapeDtypeStruct((b // bstep, groups, cap), F32)],
        scratch_shapes=[pltpu.VMEM((rows, LANES), F32), pltpu.VMEM((8, rows), F32), pltpu.VMEM((8, rows), F32)],
        compiler_params=_cparams(("arbitrary",)),
        name="select",
    )(affT.reshape(b // bstep, groups, n), affT.reshape(b // bstep, rows, LANES))
    return idx.reshape(b, N_EXPERTS, cap), gate.reshape(b, N_EXPERTS, cap)


def _dispatch_kernel(idx_ref, h_ref, xg_ref, buf, *, cap, eg):
    bi = pl.program_id(0)
    g0 = pl.program_id(1) * eg

    def per_expert(el, carry):
        base = ((bi * N_EXPERTS) + g0 + el) * cap

        def body(jo, c2):
            j0 = pl.multiple_of(jo * UNROLL, UNROLL)
            dst = buf.at[pl.ds(j0, UNROLL)]
            for u in range(UNROLL):
                t = idx_ref[base + j0 + u]
                dst[u:u + 1, :] = h_ref[0, pl.ds(t, 1), :]
            return c2

        lax.fori_loop(0, cap // UNROLL, body, 0)
        xg_ref[el, 0] = buf[...].astype(BF16)
        return carry

    lax.fori_loop(0, eg, per_expert, 0)


def _dispatch(idx, h2, cap, eg):
    b, n, _ = h2.shape
    grid_spec = pltpu.PrefetchScalarGridSpec(
        num_scalar_prefetch=1,
        grid=(b, N_EXPERTS // eg),
        in_specs=[pl.BlockSpec((1, n, D_MODEL), lambda bi, g, idx_r: (bi, 0, 0))],
        out_specs=pl.BlockSpec((eg, 1, cap, D_MODEL), lambda bi, g, idx_r: (g, bi, 0, 0)),
        scratch_shapes=[pltpu.VMEM((cap, D_MODEL), F32)],
    )
    return pl.pallas_call(
        functools.partial(_dispatch_kernel, cap=cap, eg=eg),
        grid_spec=grid_spec,
        out_shape=jax.ShapeDtypeStruct((N_EXPERTS, b, cap, D_MODEL), BF16),
        compiler_params=_cparams(("arbitrary", "arbitrary")),
        name="dispatch",
    )(idx.reshape(-1), h2)


def _combine_kernel(idx_ref, y_ref, acc_ref, *, cap, eg):
    bi = pl.program_id(0)
    g = pl.program_id(1)
    g0 = g * eg

    @pl.when(g == 0)
    def _():
        acc_ref[...] = jnp.zeros_like(acc_ref)

    def per_expert(el, carry):
        base = ((bi * N_EXPERTS) + g0 + el) * cap

        def body(jo, c2):
            j0 = pl.multiple_of(jo * UNROLL, UNROLL)
            toks = [idx_ref[base + j0 + u] for u in range(UNROLL)]
            olds = [acc_ref[0, pl.ds(t, 1), :] for t in toks]
            src = y_ref.at[el, 0, pl.ds(j0, UNROLL)]
            for u in range(UNROLL):
                acc_ref[0, pl.ds(toks[u], 1), :] = olds[u] + src[u:u + 1, :]
            return c2

        lax.fori_loop(0, cap // UNROLL, body, 0)
        return carry

    lax.fori_loop(0, eg, per_expert, 0)


def _combine(idx, y, n, cap, eg):
    b = y.shape[1]
    grid_spec = pltpu.PrefetchScalarGridSpec(
        num_scalar_prefetch=1,
        grid=(b, N_EXPERTS // eg),
        in_specs=[pl.BlockSpec((eg, 1, cap, D_MODEL), lambda bi, g, idx_r: (g, bi, 0, 0))],
        out_specs=pl.BlockSpec((1, n, D_MODEL), lambda bi, g, idx_r: (bi, 0, 0)),
    )
    return pl.pallas_call(
        functools.partial(_combine_kernel, cap=cap, eg=eg),
        grid_spec=grid_spec,
        out_shape=jax.ShapeDtypeStruct((b, n, D_MODEL), F32),
        compiler_params=_cparams(("arbitrary", "arbitrary")),
        name="combine",
    )(idx.reshape(-1), y)


def _ffn_kernel(xa_ref, ga_ref, xb_ref, gb_ref, wg_ref, wu_ref, wd_ref, ya_ref, yb_ref, wg_s, wu_s, wd_s):
    m = pl.program_id(1)

    @pl.when(m == 0)
    def _():
        wg_s[...] = wg_ref[0, 0].astype(BF16)
        wu_s[...] = wu_ref[0, 0].astype(BF16)
        wd_s[...] = wd_ref[0, 0].astype(BF16)

    def swiglu(xg_ref, gate_ref, y_ref):
        rows = xg_ref.shape[1]
        step = min(FFN_SUB, rows)
        for r0 in range(0, rows, step):
            rsl = slice(r0, r0 + step)
            x = xg_ref[0, rsl, :]
            a = _dot(x, wg_s[...])
            u = _dot(x, wu_s[...])
            hmid = (a * (1.0 / (1.0 + jnp.exp(-a))) * u).astype(BF16)
            y_ref[0, rsl, :] = _dot(hmid, wd_s[...]) * gate_ref[0, rsl, :]

    @pl.when(m == 0)
    def _():
        swiglu(xa_ref, ga_ref, ya_ref)

    @pl.when(m > 0)
    def _():
        swiglu(xb_ref, gb_ref, yb_ref)


def _ffn(xa, ga, xb, gb, w_gate, w_up, w_down, layer):
    _, ra, _ = xa.shape
    _, rb, _ = xb.shape
    tm = min(FFN_TM, rb)
    w_spec = pl.BlockSpec((1, 1, D_MODEL, D_MODEL), lambda e, m: (layer, e, 0, 0))
    b_tile = lambda e, m: (e, jnp.maximum(m - 1, 0), 0)
    return pl.pallas_call(
        _ffn_kernel,
        grid=(N_EXPERTS, 1 + rb // tm),
        in_specs=[
            pl.BlockSpec((1, ra, D_MODEL), lambda e, m: (e, 0, 0)),
            pl.BlockSpec((1, ra, 1), lambda e, m: (e, 0, 0)),
            pl.BlockSpec((1, tm, D_MODEL), b_tile),
            pl.BlockSpec((1, tm, 1), b_tile),
            w_spec, w_spec, w_spec,
        ],
        out_specs=[pl.BlockSpec((1, ra, D_MODEL), lambda e, m: (e, 0, 0)),
                   pl.BlockSpec((1, tm, D_MODEL), b_tile)],
        out_shape=[jax.ShapeDtypeStruct((N_EXPERTS, ra, D_MODEL), F32),
                   jax.ShapeDtypeStruct((N_EXPERTS, rb, D_MODEL), F32)],
        scratch_shapes=[pltpu.VMEM((D_MODEL, D_MODEL), BF16)] * 3,
        compiler_params=_cparams(("arbitrary", "arbitrary")),
        name="expert_ffn",
    )(xa, ga, xb, gb, w_gate, w_up, w_down)


def _route(h2, affT, eg):
    b, n, _ = h2.shape
    cap = (CAPACITY_FACTOR * n) // N_EXPERTS
    idx, gate = _select(affT, cap)
    xg = _dispatch(idx, h2, cap, eg)
    gate_rows = jnp.swapaxes(gate, 0, 1).reshape(N_EXPERTS, b * cap, 1)
    return idx, xg.reshape(N_EXPERTS, b * cap, D_MODEL), gate_rows, (b, n, cap, eg)


def _moe_pair(h2_a, affT_a, h2_b, affT_b, w_gate, w_up, w_down, layer):
    idx_a, xa, ga, (ba, na, cap_a, eg_a) = _route(h2_a, affT_a, N_EXPERTS)
    idx_b, xb, gb, (bb, nb, cap_b, eg_b) = _route(h2_b, affT_b, 1)
    ya, yb = _ffn(xa, ga, xb, gb, w_gate, w_up, w_down, layer)
    acc_a = _combine(idx_a, ya.reshape(N_EXPERTS, ba, cap_a, D_MODEL), na, cap_a, eg_a)
    acc_b = _combine(idx_b, yb.reshape(N_EXPERTS, bb, cap_b, D_MODEL), nb, cap_b, eg_b)
    return acc_a, acc_b


def _residual_kernel(x_ref, acc_ref, mod_ref, out_ref):
    out_ref[0] = x_ref[0] + mod_ref[0, 5:6, :] * acc_ref[0]


def _residual(x, acc, mod, mod_row):
    b, n, _ = x.shape
    tok_spec = pl.BlockSpec((1, TOK, D_MODEL), lambda bi, t: (bi, t, 0))
    return pl.pallas_call(
        _residual_kernel,
        grid=(b, n // TOK),
        in_specs=[tok_spec, tok_spec, pl.BlockSpec((1, 6, D_MODEL), lambda bi, t: (mod_row(bi), 0, 0))],
        out_specs=tok_spec,
        out_shape=jax.ShapeDtypeStruct((b, n, D_MODEL), F32),
        compiler_params=_cparams(("arbitrary", "arbitrary")),
        name="ffn_residual",
    )(x, acc, mod)


def _rope_tables(n):
    t = jnp.arange(n)
    inv_freq = ROPE_THETA ** (-jnp.arange(AXIS_PAIRS, dtype=F32) / AXIS_PAIRS)
    ang_r = (t // GRID_W).astype(F32)[None, :] * inv_freq[:, None]
    ang_c = (t % GRID_W).astype(F32)[None, :] * inv_freq[:, None]
    ang = jnp.concatenate([ang_r, ang_c], axis=0)
    return jnp.cos(ang), jnp.sin(ang)


def _band_chunk_starts(nt):
    return (0, None, nt - 3)


def _window_bias(nt):
    u = jnp.arange(TOK)[:, None]
    q = jnp.arange(TOK)[None, :]
    variants = []
    for rel0 in (0, -1, -2):
        slots = []
        for slot in range(3):
            d = (rel0 + slot) * TOK + u - q
            slots.append(jnp.where(jnp.abs(d) <= WINDOW, 0.0, NEG_INF).astype(F32))
        variants.append(jnp.stack(slots))
    return jnp.stack(variants)[:, None]


def _neighborhood_bias(rpb, rows):
    nt = rows // ROWS_PER_TOK
    kc = jnp.arange(GRID_W)[:, None]
    qc = jnp.arange(GRID_W)[None, :]
    cs = jnp.clip(qc - KW // 2, 0, GRID_W - KW)
    col_ok = (kc >= cs) & (kc < cs + KW)
    diag = (kc - qc + (KW - 1))[None] == jnp.arange(2 * KW - 1)[:, None, None]
    toep = jnp.einsum("hdj,jkq->hdkq", rpb.astype(F32) * LOG2E, diag.astype(F32),
                      precision=lax.Precision.HIGHEST)
    toep = jnp.where(col_ok, toep, NEG_INF)
    masked = jnp.full((rpb.shape[0], GRID_W, GRID_W), NEG_INF, F32)
    variants = []
    for j, chunk0 in ((0, 0), (1, 0), (nt - 1, nt - 3)):
        slots = []
        for slot in range(3):
            block_rows = []
            for a in range(ROWS_PER_TOK):
                kr = (chunk0 + slot) * ROWS_PER_TOK + a
                blocks = []
                for b in range(ROWS_PER_TOK):
                    qr = j * ROWS_PER_TOK + b
                    rs = min(max(qr - KH // 2, 0), rows - KH)
                    ok = rs <= kr < rs + KH
                    blocks.append(toep[:, kr - qr + (KH - 1)] if ok else masked)
                block_rows.append(jnp.concatenate(blocks, axis=-1))
            slots.append(jnp.concatenate(block_rows, axis=-2))
        variants.append(jnp.stack(slots, axis=1))
    return jnp.stack(variants)


def kernel(x_prompt, x_sample, cache_k, cache_v, c, c_ctx, w_ada, b_ada, norm_attn_g, norm_ffn_g,
           w_qkv, q_norm_g, k_norm_g, w_o, sink_logits, rel_pos_bias, w_router, w_gate, w_up, w_down):
    bp, n_p, _ = x_prompt.shape
    bs, n_s, _ = x_sample.shape
    past = cache_k.shape[2]
    assert n_p % TOK == 0 and n_s % TOK == 0 and past % TOK == 0
    assert n_s // TOK >= 3 and 1 + bs <= COND_ROWS

    cond = jnp.zeros((COND_ROWS, D_MODEL), F32).at[0].set(c_ctx).at[1:1 + bs].set(c)
    mods = _ada_modulation(cond, w_ada, b_ada).reshape(DEPTH, COND_ROWS, 6, D_MODEL)
    row_p = lambda bi: 0
    row_s = lambda bi: bi + 1

    nc = past // TOK
    kc_all = jnp.transpose(cache_k, (1, 0, 3, 2, 4)).astype(BF16)
    kc_all = kc_all.reshape(DEPTH, bs, N_KV_HEADS, nc, TOK, HEAD_DIM)
    vc_all = jnp.transpose(cache_v, (1, 0, 3, 4, 2)).astype(BF16)
    vc_all = jnp.transpose(vc_all.reshape(DEPTH, bs, N_KV_HEADS, HEAD_DIM, nc, TOK), (0, 1, 2, 4, 3, 5))
    vc_all = jnp.concatenate(
        [vc_all, jnp.ones((DEPTH, bs, N_KV_HEADS, nc, VT_ROWS - HEAD_DIM, TOK), BF16)], axis=4)

    rope_tabs = _rope_tables(n_s)
    win_bias = _window_bias(n_s // TOK)

    xp, xs = x_prompt, x_sample
    acc_p = acc_s = None
    new_k, new_v = [], []
    for i in range(DEPTH):
        kind = i % 3
        sink = sink_logits[i // 3] if kind == 1 else None
        prev = mods[i - 1] if i > 0 else None

        xp, qT, k, vT, ck, cv = _pre_attn(xp, acc_p, prev, mods[i], row_p, norm_attn_g[i], w_qkv, i,
                                          q_norm_g[i], k_norm_g[i], None, True)
        new_k.append(ck.reshape(bp, n_p, N_KV_HEADS, HEAD_DIM))
        new_v.append(cv.reshape(bp, n_p, N_KV_HEADS, HEAD_DIM))
        oT = _attention(qT, k, vT, None, sink, None)
        xp, h2_p, affT_p = _post_attn(oT, xp, mods[i], row_p, norm_ffn_g[i], w_o, i, w_router[i], True)

        tabs = rope_tabs if kind != 2 else None
        xs, qT, k, vT = _pre_attn(xs, acc_s, prev, mods[i], row_s, norm_attn_g[i], w_qkv, i,
                                  q_norm_g[i], k_norm_g[i], tabs, False)
        ctx = (kc_all[i], vc_all[i])
        if kind == 0:
            oT = _attention(qT, k, vT, ctx, None, None)
        elif kind == 1:
            oT = _attention(qT, k, vT, ctx, sink, win_bias)
        else:
            oT = _attention(qT, k, vT, ctx, None, _neighborhood_bias(rel_pos_bias[i // 3], n_s // GRID_W))
        xs, h2_s, affT_s = _post_attn(oT, xs, mods[i], row_s, norm_ffn_g[i], w_o, i, w_router[i], False)
        acc_p, acc_s = _moe_pair(h2_p, affT_p, h2_s, affT_s, w_gate, w_up, w_down, i)

    y_prompt = _residual(xp, acc_p, mods[DEPTH - 1], row_p)
    y_sample = _residual(xs, acc_s, mods[DEPTH - 1], row_s)
    return (y_prompt, y_sample, jnp.stack(new_k, axis=1), jnp.stack(new_v, axis=1))
```
